```python
import math
import jax, jax.numpy as jnp
from jax import lax
import numpy as np

D_MODEL = 2048
BATCH = 4
SEQ = 4096
DEPTH = 2

CHUNK = 64
D_MIX = D_MODEL
SSM_DIM = D_MIX // 4
SSM_GROUP_CH = 16
SSM_GROUPS = SSM_DIM // SSM_GROUP_CH
SSM_STATE = 64
CONV_DIM = D_MIX // 4
CONV_WIDTH = 3
ATTN_DIM = D_MIX - SSM_DIM - CONV_DIM
HEAD_DIM = 128
N_HEADS = ATTN_DIM // HEAD_DIM
IDX_HEADS = 16
IDX_DIM = 64
TOPK_MAX = 256
Q_BLOCK = CHUNK
D_FF = 4 * D_MODEL
ROPE_THETA = 10000.0
ALPHA = (2.0 * DEPTH) ** 0.25
BETA = (8.0 * DEPTH) ** -0.25
LN_EPS = 1e-5
RMS_EPS = 1e-6
DT_MIN = 1e-3
DT_MAX = 1e-1
IN_SPLITS = (SSM_DIM,
             CONV_DIM, CONV_DIM, CONV_DIM,
             ATTN_DIM, ATTN_DIM, ATTN_DIM,
             IDX_HEADS * IDX_DIM,
             IDX_DIM,
             IDX_HEADS)
D_IN = sum(IN_SPLITS)

kernel_name = "hybrid_s5_shortconv_dsa_deepnorm_adaln"


def layer_norm(x, g, b):
    xf = x.astype(jnp.float32)
    mu = jnp.mean(xf, axis=-1, keepdims=True)
    var = jnp.mean(jnp.square(xf - mu), axis=-1, keepdims=True)
    return ((xf - mu) * lax.rsqrt(var + LN_EPS) * g.astype(jnp.float32)
            + b.astype(jnp.float32)).astype(x.dtype)


def rms_norm(x, g):
    xf = x.astype(jnp.float32)
    y = xf * lax.rsqrt(jnp.mean(jnp.square(xf), axis=-1, keepdims=True) + RMS_EPS)
    return (y * g.astype(jnp.float32)).astype(x.dtype)


def rope_tables(seq, dim):
    inv = 1.0 / (ROPE_THETA ** (jnp.arange(0, dim, 2, dtype=jnp.float32) / dim))
    ang = jnp.arange(seq, dtype=jnp.float32)[:, None] * inv[None, :]
    return jnp.cos(ang), jnp.sin(ang)


def apply_rope(x, cos, sin):
    shp = (cos.shape[0],) + (1,) * (x.ndim - 3) + (cos.shape[1],)
    cos = cos.reshape(shp).astype(x.dtype)
    sin = sin.reshape(shp).astype(x.dtype)
    x1, x2 = jnp.split(x, 2, axis=-1)
    return jnp.concatenate([x1 * cos - x2 * sin, x2 * cos + x1 * sin], axis=-1)


def s5_mixer(u, lam_re, lam_im, log_dt, b_re, b_im, c_re, c_im, d_skip, w_glu, b_glu):
    bsz, seq, _ = u.shape
    f32 = jnp.float32
    uf = u.astype(f32).reshape(bsz, seq, SSM_GROUPS, SSM_GROUP_CH)
    lr, li = lam_re.astype(f32), lam_im.astype(f32)
    dt = jnp.exp(log_dt.astype(f32))[:, None]
    mag = jnp.exp(lr * dt)
    ang = li * dt
    lb_re, lb_im = mag * jnp.cos(ang), mag * jnp.sin(ang)
    den = lr * lr + li * li
    n_re, n_im = lb_re - 1.0, lb_im
    f_re = (n_re * lr + n_im * li) / den
    f_im = (n_im * lr - n_re * li) / den
    br, bi = b_re.astype(f32), b_im.astype(f32)
    bb_re = f_re[..., None] * br - f_im[..., None] * bi
    bb_im = f_re[..., None] * bi + f_im[..., None] * br
    bu_re = jnp.einsum('bsgh,gph->bsgp', uf, bb_re)
    bu_im = jnp.einsum('bsgh,gph->bsgp', uf, bb_im)
    a_re = jnp.broadcast_to(lb_re, bu_re.shape)
    a_im = jnp.broadcast_to(lb_im, bu_im.shape)

    def combine(left, right):
        a1r, a1i, b1r, b1i = left
        a2r, a2i, b2r, b2i = right
        return (a2r * a1r - a2i * a1i,
                a2r * a1i + a2i * a1r,
                a2r * b1r - a2i * b1i + b2r,
                a2r * b1i + a2i * b1r + b2i)

    _, _, xr, xi = lax.associative_scan(combine, (a_re, a_im, bu_re, bu_im), axis=1)
    y = (jnp.einsum('bsgp,ghp->bsgh', xr, c_re.astype(f32))
         - jnp.einsum('bsgp,ghp->bsgh', xi, c_im.astype(f32)))
    y = y.reshape(bsz, seq, SSM_DIM) + d_skip.astype(f32) * uf.reshape(bsz, seq, SSM_DIM)
    y = y.astype(u.dtype)
    g = jax.nn.gelu(y)
    return g * jax.nn.sigmoid(g @ w_glu + b_glu)


def short_conv_mixer(h, gate_b, gate_c, conv_w):
    z = gate_c * h
    z = lax.conv_general_dilated(z, conv_w[:, None, :].astype(z.dtype), window_strides=(1,),
                                 padding=[(CONV_WIDTH - 1, 0)],
                                 dimension_numbers=('NWC', 'WIO', 'NWC'),
                                 feature_group_count=CONV_DIM)
    return gate_b * z


def sparse_attention(q, k, v, qi, ki, wi):
    bsz, seq = q.shape[0], q.shape[1]
    topk = min(TOPK_MAX, seq // 4)
    nqb = seq // Q_BLOCK
    key_chunk = jnp.arange(seq) // CHUNK

    def to_blocks(a):
        a = a.reshape((bsz, nqb, Q_BLOCK) + a.shape[2:])
        return jnp.moveaxis(a, 1, 0)

    def block(args):
        qb, qib, wib, j = args
        q_chunk = (j * Q_BLOCK + jnp.arange(Q_BLOCK)) // CHUNK
        logits = jnp.einsum('bqhd,bsd->bqsh', qib, ki) * (IDX_DIM ** -0.5)
        score = jnp.einsum('bqsh,bqh->bqs', jax.nn.relu(logits), wib) * (IDX_HEADS ** -0.5)
        adm = key_chunk[None, :] <= q_chunk[:, None]
        score = jnp.where(adm[None], score.astype(jnp.float32), -jnp.inf)
        _, idx = lax.top_k(score, topk)
        valid = (idx // CHUNK) <= q_chunk[None, :, None]
        kg = jax.vmap(lambda kk, ii: kk[ii])(k, idx)
        vg = jax.vmap(lambda vv, ii: vv[ii])(v, idx)
        s = jnp.einsum('bqhd,bqkhd->bhqk', qb, kg).astype(jnp.float32) * (HEAD_DIM ** -0.5)
        s = jnp.where(valid[:, None], s, -jnp.inf)
        p = jax.nn.softmax(s, axis=-1).astype(vg.dtype)
        return jnp.einsum('bhqk,bqkhd->bqhd', p, vg)

    out = lax.map(block, (to_blocks(q), to_blocks(qi), to_blocks(wi), jnp.arange(nqb)))
    out = jnp.moveaxis(out, 0, 1).reshape(bsz, seq, N_HEADS * HEAD_DIM)
    return out


def token_mixer(h, w_in, lam_re, lam_im, log_dt, b_re, b_im, c_re, c_im, d_skip,
                w_glu, b_glu, conv_w, gnorm_g, w_o, cos_a, sin_a, cos_i, sin_i):
    bsz, seq, _ = h.shape
    proj = h @ w_in
    offsets = [int(o) for o in np.cumsum(IN_SPLITS)[:-1]]
    u, ch, cb, cc, q, k, v, qi, ki, wi = jnp.split(proj, offsets, axis=-1)
    y_ssm = s5_mixer(u, lam_re, lam_im, log_dt, b_re, b_im, c_re, c_im, d_skip, w_glu, b_glu)
    y_conv = short_conv_mixer(ch, cb, cc, conv_w)
    q = apply_rope(q.reshape(bsz, seq, N_HEADS, HEAD_DIM), cos_a, sin_a)
    k = apply_rope(k.reshape(bsz, seq, N_HEADS, HEAD_DIM), cos_a, sin_a)
    v = v.reshape(bsz, seq, N_HEADS, HEAD_DIM)
    qi = apply_rope(qi.reshape(bsz, seq, IDX_HEADS, IDX_DIM), cos_i, sin_i)
    ki = apply_rope(ki, cos_i, sin_i)
    y_attn = sparse_attention(q, k, v, qi, ki, wi)
    g_ssm, g_conv, g_attn = jnp.split(gnorm_g, [SSM_DIM, SSM_DIM + CONV_DIM])
    y = jnp.concatenate([rms_norm(y_ssm, g_ssm), rms_norm(y_conv, g_conv),
                         rms_norm(y_attn, g_attn)], axis=-1)
    return y @ w_o


def setup_inputs(seed: int = 0) -> dict:
    key = jax.random.key(seed)
    ks = jax.random.split(key, 32)
    f32 = jnp.float32
    L = DEPTH

    def nrm(k, shape, s):
        return jax.random.normal(k, shape, f32) * s

    n_idx = jnp.arange(SSM_STATE, dtype=f32)
    lam_re = -0.5 * (1.0 + nrm(ks[5], (L, SSM_GROUPS, SSM_STATE), 0.01))
    lam_im = (math.pi * n_idx * (1.0 + nrm(ks[6], (L, SSM_GROUPS, SSM_STATE), 0.01))
              + nrm(ks[7], (L, SSM_GROUPS, SSM_STATE), 0.01))
    log_dt = jax.random.uniform(ks[8], (L, SSM_GROUPS), f32,
                                math.log(DT_MIN), math.log(DT_MAX))
    return {
        "x": nrm(ks[0], (BATCH, SEQ, D_MODEL), 1.0),
        "c": nrm(ks[1], (BATCH, D_MODEL), 1.0),
        "w_ada": nrm(ks[2], (L, D_MODEL, 6 * D_MODEL), 0.5 * D_MODEL ** -0.5),
        "b_ada": nrm(ks[3], (L, 6 * D_MODEL), 0.01),
        "w_in": nrm(ks[4], (L, D_MODEL, D_IN), D_MODEL ** -0.5),
        "lam_re": lam_re,
        "lam_im": lam_im,
        "log_dt": log_dt,
        "ssm_b_re": nrm(ks[9], (L, SSM_GROUPS, SSM_STATE, SSM_GROUP_CH), (2 * SSM_GROUP_CH) ** -0.5),
        "ssm_b_im": nrm(ks[10], (L, SSM_GROUPS, SSM_STATE, SSM_GROUP_CH), (2 * SSM_GROUP_CH) ** -0.5),
        "ssm_c_re": nrm(ks[11], (L, SSM_GROUPS, SSM_GROUP_CH, SSM_STATE), (2 * SSM_STATE) ** -0.5),
        "ssm_c_im": nrm(ks[12], (L, SSM_GROUPS, SSM_GROUP_CH, SSM_STATE), (2 * SSM_STATE) ** -0.5),
        "ssm_d": nrm(ks[13], (L, SSM_DIM), 1.0),
        "w_glu": nrm(ks[14], (L, SSM_DIM, SSM_DIM), SSM_DIM ** -0.5),
        "b_glu": nrm(ks[15], (L, SSM_DIM), 0.01),
        "conv_w": nrm(ks[16], (L, CONV_WIDTH, CONV_DIM), CONV_WIDTH ** -0.5),
        "gnorm_g": 1.0 + nrm(ks[17], (L, D_MIX), 0.01),
        "w_o": nrm(ks[18], (L, D_MIX, D_MODEL), BETA * D_MIX ** -0.5),
        "ln1_g": 1.0 + nrm(ks[19], (L, D_MODEL), 0.01),
        "ln1_b": nrm(ks[20], (L, D_MODEL), 0.01),
        "w_ff1": nrm(ks[21], (L, D_MODEL, D_FF), D_MODEL ** -0.5),
        "w_ff2": nrm(ks[22], (L, D_FF, D_MODEL), BETA * D_FF ** -0.5),
        "ln2_g": 1.0 + nrm(ks[23], (L, D_MODEL), 0.01),
        "ln2_b": nrm(ks[24], (L, D_MODEL), 0.01),
    }


def reference(x, c, w_ada, b_ada, w_in, lam_re, lam_im, log_dt, ssm_b_re, ssm_b_im,
              ssm_c_re, ssm_c_im, ssm_d, w_glu, b_glu, conv_w, gnorm_g, w_o,
              ln1_g, ln1_b, w_ff1, w_ff2, ln2_g, ln2_b):
    seq = x.shape[1]
    cos_a, sin_a = rope_tables(seq, HEAD_DIM)
    cos_i, sin_i = rope_tables(seq, IDX_DIM)
    for l in range(DEPTH):
        mod = c @ w_ada[l] + b_ada[l]
        sh1, sc1, g1, sh2, sc2, g2 = [m[:, None, :] for m in jnp.split(mod, 6, axis=-1)]
        h = x * (1.0 + sc1) + sh1
        mix = token_mixer(h, w_in[l], lam_re[l], lam_im[l], log_dt[l], ssm_b_re[l], ssm_b_im[l],
                          ssm_c_re[l], ssm_c_im[l], ssm_d[l], w_glu[l], b_glu[l], conv_w[l],
                          gnorm_g[l], w_o[l], cos_a, sin_a, cos_i, sin_i)
        x = layer_norm(ALPHA * x + g1 * mix, ln1_g[l], ln1_b[l])
        h = x * (1.0 + sc2) + sh2
        ff = jnp.square(jax.nn.relu(h @ w_ff1[l])) @ w_ff2[l]
        x = layer_norm(ALPHA * x + g2 * ff, ln2_g[l], ln2_b[l])
    return x
```

```python
import functools
import math

import jax
import jax.numpy as jnp
from jax import lax
from jax.experimental import pallas as pl
from jax.experimental.pallas import tpu as pltpu

D_MODEL = 2048
DEPTH = 2
CHUNK = 64
SSM_DIM = 512
SSM_GROUP_CH = 16
SSM_GROUPS = 32
SSM_STATE = 64
CONV_DIM = 512
CONV_WIDTH = 3
ATTN_DIM = 1024
HEAD_DIM = 128
N_HEADS = 8
IDX_HEADS = 16
IDX_DIM = 64
TOPK_MAX = 256
D_FF = 4 * D_MODEL
ROPE_THETA = 10000.0
ALPHA = (2.0 * DEPTH) ** 0.25
LN_EPS = 1e-5
RMS_EPS = 1e-6

LANES = 128
SUBLANES = 8
VMEM_LIMIT = 56 * 1024 * 1024

ADA_TN = 1024
PROJ_TM = 256
S5_TC = 128
S5_HALF = SSM_DIM // 2
S5_NH = SSM_GROUPS // 2 * SSM_STATE
CONV_TM = 512
ATT_TQ = 256
ATT_KB = 256
OUT_TM = 512
FFN_TM = 512
FFN_TF = 1024

BF16 = jnp.bfloat16
F32 = jnp.float32
NEG_BIG = -1e30


def _cparams(sem):
    return pltpu.CompilerParams(dimension_semantics=sem, vmem_limit_bytes=VMEM_LIMIT)


def _resident(shape):
    nd = len(shape)
    return pl.BlockSpec(shape, lambda *_: (0,) * nd, pipeline_mode=pl.Buffered(1))


def _adaln_kernel(c_ref, w_ref, b_ref, o_ref):
    w = w_ref[0].astype(BF16)
    o_ref[0] = jnp.dot(c_ref[...], w, preferred_element_type=F32) + b_ref[0]


def adaln(c, w_ada, b_ada):
    nb = c.shape[0]
    rows = 16
    cp = jnp.zeros((rows, D_MODEL), BF16).at[:nb].set(c.astype(BF16))
    n_out = w_ada.shape[-1]
    out = pl.pallas_call(
        _adaln_kernel,
        grid=(DEPTH, n_out // ADA_TN),
        in_specs=[
            pl.BlockSpec((rows, D_MODEL), lambda l, j: (0, 0)),
            pl.BlockSpec((1, D_MODEL, ADA_TN), lambda l, j: (l, 0, j)),
            pl.BlockSpec((1, 1, ADA_TN), lambda l, j: (l, 0, j)),
        ],
        out_specs=pl.BlockSpec((1, rows, ADA_TN), lambda l, j: (l, 0, j)),
        out_shape=jax.ShapeDtypeStruct((DEPTH, rows, n_out), F32),
        compiler_params=_cparams(("arbitrary", "arbitrary")),
        name="adaln",
    )(cp, w_ada, b_ada.reshape(DEPTH, 1, n_out))
    return out[:, :nb]


def _rope_halves(x, cos, sin_signed, half):
    if 2 * half == LANES:
        swapped = pltpu.roll(x, half, axis=1)
    else:
        lane = lax.broadcasted_iota(jnp.int32, x.shape, 1)
        first = (lane % (2 * half)) < half
        swapped = jnp.where(first, pltpu.roll(x, LANES - half, axis=1), pltpu.roll(x, half, axis=1))
    return x * cos + swapped * sin_signed


def _in_proj_kernel(x_ref, sc_ref, sh_ref, wu_ref, wc_ref, wq_ref, wk_ref, wv_ref, wqi_ref, wt_ref,
                    cosa_ref, sina_ref, cosi_ref, sini_ref,
                    u_ref, conv_ref, q_ref, k_ref, v_ref, qi_ref, kk_ref, wi_ref):
    h = (x_ref[...] * (1.0 + sc_ref[0]) + sh_ref[0]).astype(BF16)
    u_ref[...] = jnp.dot(h, wu_ref[...], preferred_element_type=F32)
    conv_ref[...] = jnp.dot(h, wc_ref[...], preferred_element_type=F32)
    v_ref[...] = jnp.dot(h, wv_ref[...], preferred_element_type=F32).astype(BF16)

    cosa, sina = cosa_ref[...], sina_ref[...]
    q = jnp.dot(h, wq_ref[...], preferred_element_type=F32)
    k = jnp.dot(h, wk_ref[...], preferred_element_type=F32)
    for hd in range(N_HEADS):
        sl = slice(hd * HEAD_DIM, (hd + 1) * HEAD_DIM)
        q_ref[:, sl] = (_rope_halves(q[:, sl], cosa, sina, HEAD_DIM // 2)
                        * (HEAD_DIM ** -0.5)).astype(BF16)
        k_ref[:, sl] = _rope_halves(k[:, sl], cosa, sina, HEAD_DIM // 2).astype(BF16)

    cosi, sini = cosi_ref[...], sini_ref[...]
    qi = jnp.dot(h, wqi_ref[...], preferred_element_type=F32)
    for g in range(IDX_HEADS * IDX_DIM // LANES):
        sl = slice(g * LANES, (g + 1) * LANES)
        qi_ref[:, sl] = _rope_halves(qi[:, sl], cosi, sini, IDX_DIM // 2).astype(BF16)

    tail = jnp.dot(h, wt_ref[...], preferred_element_type=F32)
    ki2 = jnp.where(lax.broadcasted_iota(jnp.int32, tail.shape, 1) < IDX_DIM,
                    tail, pltpu.roll(tail, IDX_DIM, axis=1))
    kk_ref[...] = _rope_halves(ki2, cosi, sini, IDX_DIM // 2).astype(BF16)
    wi_ref[...] = pltpu.roll(tail, LANES - IDX_DIM, axis=1) * (
        (IDX_DIM ** -0.5) * (IDX_HEADS ** -0.5))


def in_proj(x2, sc, sh, w_in, tables, seq):
    n = x2.shape[0]
    tm = PROJ_TM
    tiles_per_seq = seq // tm
    wb = w_in.astype(BF16)
    o = [0, SSM_DIM, SSM_DIM + 3 * CONV_DIM]
    o += [o[-1] + ATTN_DIM, o[-1] + 2 * ATTN_DIM, o[-1] + 3 * ATTN_DIM]
    o += [o[-1] + IDX_HEADS * IDX_DIM]
    wu, wc, wq, wk, wv, wqi = (wb[:, o[i]:o[i + 1]] for i in range(6))
    tail_w = IDX_DIM + IDX_HEADS
    wt = jnp.zeros((D_MODEL, LANES), BF16).at[:, :tail_w].set(wb[:, o[6]:o[6] + tail_w])
    cosa, sina, cosi, sini = tables

    row = lambda w: pl.BlockSpec((tm, w), lambda i: (i, 0))
    mod = pl.BlockSpec((1, 1, D_MODEL), lambda i: (i // tiles_per_seq, 0, 0))
    tab = pl.BlockSpec((tm, LANES), lambda i: (i % tiles_per_seq, 0))
    outs = pl.pallas_call(
        _in_proj_kernel,
        grid=(n // tm,),
        in_specs=[row(D_MODEL), mod, mod,
                  _resident(wu.shape), _resident(wc.shape), _resident(wq.shape), _resident(wk.shape),
                  _resident(wv.shape), _resident(wqi.shape), _resident(wt.shape),
                  tab, tab, tab, tab],
        out_specs=[row(SSM_DIM), row(3 * CONV_DIM), row(ATTN_DIM), row(ATTN_DIM), row(ATTN_DIM),
                   row(IDX_HEADS * IDX_DIM), row(LANES), row(LANES)],
        out_shape=[jax.ShapeDtypeStruct((n, SSM_DIM), F32),
                   jax.ShapeDtypeStruct((n, 3 * CONV_DIM), F32),
                   jax.ShapeDtypeStruct((n, ATTN_DIM), BF16),
                   jax.ShapeDtypeStruct((n, ATTN_DIM), BF16),
                   jax.ShapeDtypeStruct((n, ATTN_DIM), BF16),
                   jax.ShapeDtypeStruct((n, IDX_HEADS * IDX_DIM), BF16),
                   jax.ShapeDtypeStruct((n, LANES), BF16),
                   jax.ShapeDtypeStruct((n, LANES), F32)],
        compiler_params=_cparams(("arbitrary",)),
        name="in_proj",
    )(x2, sc, sh, wu, wc, wq, wk, wv, wqi, wt, cosa, sina, cosi, sini)
    return outs


def rope_tables(seq):
    def tab(dim):
        inv = 1.0 / (ROPE_THETA ** (jnp.arange(0, dim, 2, dtype=F32) / dim))
        ang = jnp.arange(seq, dtype=F32)[:, None] * inv[None, :]
        cos, sin = jnp.cos(ang), jnp.sin(ang)
        reps = LANES // dim
        return (jnp.tile(jnp.concatenate([cos, cos], axis=1), (1, reps)),
                jnp.tile(jnp.concatenate([-sin, sin], axis=1), (1, reps)))
    cosa, sina = tab(HEAD_DIM)
    cosi, sini = tab(IDX_DIM)
    return cosa, sina, cosi, sini


def _s5_kernel(lhs_ref, utb_ref, bb_ref, cc_ref, are_ref, aim_ref, d_ref, wg_ref, bg_ref, gn_ref,
               o_ref, st_ref, xs_ref, y_ref, *, nb):
    rows_t = 2 * nb
    tc = xs_ref.shape[0] // rows_t

    @pl.when(pl.program_id(0) == 0)
    def _():
        st_ref[...] = jnp.zeros_like(st_ref)

    xs_ref[...] = jnp.dot(lhs_ref[...], bb_ref[...], preferred_element_type=F32)
    a_re, a_im = are_ref[...], aim_ref[...]

    def step(t, carry):
        xr, xi = carry
        r0 = pl.multiple_of(t * rows_t, rows_t)
        bur = xs_ref[pl.ds(r0, rows_t), :S5_NH]
        bui = xs_ref[pl.ds(r0, rows_t), S5_NH:]
        nr = a_re * xr - a_im * xi + bur
        ni = a_re * xi + a_im * xr + bui
        xs_ref[pl.ds(r0, rows_t), :S5_NH] = nr
        xs_ref[pl.ds(r0, rows_t), S5_NH:] = ni
        return nr, ni

    xr, xi = lax.fori_loop(0, tc, step, (st_ref[0], st_ref[1]), unroll=4)
    st_ref[0] = xr
    st_ref[1] = xi

    n_slab = SSM_DIM // LANES
    yy = jnp.dot(xs_ref[...].astype(BF16), cc_ref[...], preferred_element_type=F32)
    for s in range(n_slab):
        y_ref[s] = yy[:, s * LANES:(s + 1) * LANES]
    n_tb = tc * nb
    y = jnp.concatenate([y_ref[s, pl.ds(s // (n_slab // 2), n_tb, stride=2), :]
                         for s in range(n_slab)], axis=1)
    y = y + d_ref[...] * utb_ref[...]
    g = jax.nn.gelu(y)
    z = jnp.dot(g.astype(BF16), wg_ref[...], preferred_element_type=F32) + bg_ref[...]
    out = g * jax.nn.sigmoid(z)
    out = out * lax.rsqrt(jnp.mean(out * out, axis=-1, keepdims=True) + RMS_EPS) * gn_ref[...]
    for s in range(n_slab):
        y_ref[s, pl.ds(0, n_tb), :] = out[:, s * LANES:(s + 1) * LANES]
    for b in range(nb):
        for s in range(n_slab):
            o_ref[b, :, s * LANES:(s + 1) * LANES] = (
                y_ref[s, pl.ds(b, tc, stride=nb), :].astype(BF16))


def s5_group(u, lam_re, lam_im, log_dt, b_re, b_im, c_re, c_im, d_skip, w_glu, b_glu, gn):
    nb, seq, _ = u.shape
    assert 2 * nb == SUBLANES
    tc = S5_TC
    dt = jnp.exp(log_dt)[:, None]
    mag = jnp.exp(lam_re * dt)
    ang = lam_im * dt
    lb_re, lb_im = mag * jnp.cos(ang), mag * jnp.sin(ang)
    den = lam_re * lam_re + lam_im * lam_im
    n_re, n_im = lb_re - 1.0, lb_im
    f_re = (n_re * lam_re + n_im * lam_im) / den
    f_im = (n_im * lam_re - n_re * lam_im) / den
    bb_re = f_re[..., None] * b_re - f_im[..., None] * b_im
    bb_im = f_re[..., None] * b_im + f_im[..., None] * b_re
    gh = SSM_GROUPS // 2
    eye = jnp.eye(gh, dtype=F32)

    def in_mat(m):
        m = m.reshape(2, gh, SSM_STATE, SSM_GROUP_CH)
        bd = jnp.einsum('rgph,gk->rghkp', m, eye)
        return bd.reshape(SSM_DIM, S5_NH)

    def out_mat(m):
        m = m.reshape(2, gh, SSM_GROUP_CH, SSM_STATE)
        bd = jnp.einsum('rghp,gk->kprgh', m, eye)
        return bd.reshape(S5_NH, SSM_DIM)

    bb = jnp.concatenate([in_mat(bb_re), in_mat(bb_im)], axis=1).astype(BF16)
    cc = jnp.concatenate([out_mat(c_re), -out_mat(c_im)], axis=0).astype(BF16)

    def lane_vec(m):
        return jnp.tile(m.reshape(2, S5_NH), (nb, 1))

    a_re, a_im = lane_vec(lb_re), lane_vec(lb_im)

    utb = jnp.swapaxes(u, 0, 1)
    half_mask = (jnp.arange(SSM_DIM)[None, :] // S5_HALF) == jnp.arange(2)[:, None]
    lhs = jnp.where(half_mask[None, None], utb[:, :, None, :], 0.0).astype(BF16)
    lhs = lhs.reshape(seq * 2 * nb, SSM_DIM)
    utb = utb.reshape(seq * nb, SSM_DIM)

    rows = tc * 2 * nb
    out = pl.pallas_call(
        functools.partial(_s5_kernel, nb=nb),
        grid=(seq // tc,),
        in_specs=[
            pl.BlockSpec((rows, SSM_DIM), lambda i: (i, 0)),
            pl.BlockSpec((tc * nb, SSM_DIM), lambda i: (i, 0)),
            _resident(bb.shape), _resident(cc.shape),
            _resident(a_re.shape), _resident(a_im.shape),
            _resident((1, SSM_DIM)), _resident((SSM_DIM, SSM_DIM)),
            _resident((1, SSM_DIM)), _resident((1, SSM_DIM)),
        ],
        out_specs=pl.BlockSpec((nb, tc, SSM_DIM), lambda i: (0, i, 0)),
        out_shape=jax.ShapeDtypeStruct((nb, seq, SSM_DIM), BF16),
        scratch_shapes=[pltpu.VMEM((2, 2 * nb, S5_NH), F32),
                        pltpu.VMEM((rows, 2 * S5_NH), F32),
                        pltpu.VMEM((SSM_DIM // LANES, rows, LANES), F32)],
        compiler_params=_cparams(("arbitrary",)),
        name="s5",
    )(lhs, utb, bb, cc, a_re, a_im, d_skip.reshape(1, -1), w_glu.astype(BF16),
      b_glu.reshape(1, -1), gn.reshape(1, -1))
    return out


def _conv_kernel(c_ref, w_ref, gn_ref, o_ref, zp_ref):
    tm = o_ref.shape[0]
    ch = c_ref[:, :CONV_DIM]
    gb = c_ref[:, CONV_DIM:2 * CONV_DIM]
    gc = c_ref[:, 2 * CONV_DIM:]

    @pl.when(pl.program_id(1) == 0)
    def _():
        zp_ref[pl.ds(0, SUBLANES), :] = jnp.zeros((SUBLANES, CONV_DIM), F32)

    @pl.when(pl.program_id(1) != 0)
    def _():
        zp_ref[pl.ds(0, SUBLANES), :] = zp_ref[pl.ds(tm, SUBLANES), :]

    zp_ref[pl.ds(SUBLANES, tm), :] = gc * ch
    acc = zp_ref[pl.ds(SUBLANES, tm), :] * w_ref[2:3, :]
    acc += zp_ref[pl.ds(SUBLANES - 1, tm), :] * w_ref[1:2, :]
    acc += zp_ref[pl.ds(SUBLANES - 2, tm), :] * w_ref[0:1, :]
    y = gb * acc
    y = y * lax.rsqrt(jnp.mean(y * y, axis=-1, keepdims=True) + RMS_EPS) * gn_ref[...]
    o_ref[...] = y.astype(BF16)


def conv_group(conv_in, conv_w, gn, nb, seq):
    tm = CONV_TM
    tiles = seq // tm
    return pl.pallas_call(
        _conv_kernel,
        grid=(nb, tiles),
        in_specs=[pl.BlockSpec((tm, 3 * CONV_DIM), lambda b, i: (b * tiles + i, 0)),
                  pl.BlockSpec((CONV_WIDTH, CONV_DIM), lambda b, i: (0, 0)),
                  pl.BlockSpec((1, CONV_DIM), lambda b, i: (0, 0))],
        out_specs=pl.BlockSpec((tm, CONV_DIM), lambda b, i: (b * tiles + i, 0)),
        out_shape=jax.ShapeDtypeStruct((nb * seq, CONV_DIM), BF16),
        scratch_shapes=[pltpu.VMEM((tm + SUBLANES, CONV_DIM), F32)],
        compiler_params=_cparams(("arbitrary", "arbitrary")),
        name="conv",
    )(conv_in, conv_w, gn.reshape(1, -1))


def _key_to_f32(key):
    bits = key ^ ((key >> 31) & jnp.int32(0x7FFFFFFF))
    return pltpu.bitcast(bits, F32)


def _attn_kernel(q_ref, qi_ref, wi_ref, k_ref, vt_ref, kk_ref, gn_ref, o_ref,
                 sc_ref, lhs_ref, wib_ref, m_ref, l_ref, a_ref, acc_ref, s_ref, p_ref, *, topk):
    tq, kb_sz = ATT_TQ, ATT_KB
    i = pl.program_id(1)
    n_kb = i + 1
    lane128 = lax.broadcasted_iota(jnp.int32, (tq, LANES), 1)

    for h in range(IDX_HEADS):
        pair = qi_ref[:, (h // 2) * LANES:(h // 2 + 1) * LANES]
        mine = (lane128 // IDX_DIM) == (h % 2)
        lhs_ref[h] = jnp.where(mine, pair, jnp.zeros_like(pair))
        wib_ref[h] = jnp.broadcast_to(wi_ref[:, h:h + 1], (tq, LANES))

    key_row = lax.broadcasted_iota(jnp.int32, (kb_sz, tq), 0)
    qry_lane = lax.broadcasted_iota(jnp.int32, (kb_sz, tq), 1)
    diag_adm = key_row // CHUNK <= qry_lane // CHUNK

    def score_block(kb, _):
        k0 = pl.multiple_of(kb * kb_sz, kb_sz)
        kk = kk_ref[pl.ds(k0, kb_sz), :]
        acc = jnp.zeros((tq, kb_sz), F32)
        for h in range(IDX_HEADS):
            logit = lax.dot_general(lhs_ref[h], kk, (((1,), (1,)), ((), ())),
                                    preferred_element_type=F32)
            w = wib_ref[h]
            acc = acc + jnp.maximum(logit, 0.0) * jnp.concatenate([w] * (kb_sz // LANES), axis=1)
        sc_ref[kb] = jnp.where(jnp.logical_or(kb < i, diag_adm), acc.T, -jnp.inf)
        return 0

    lax.fori_loop(0, n_kb, score_block, 0)

    def count_if(pred):
        def body(kb, cnt):
            hit = jnp.where(pred(sc_ref[kb], kb * kb_sz), 1.0, 0.0)
            return cnt + jnp.sum(hit.reshape(kb_sz // SUBLANES, SUBLANES, tq), axis=0)
        cnt = lax.fori_loop(0, n_kb, body, jnp.zeros((SUBLANES, tq), F32))
        return jnp.sum(cnt, axis=0, keepdims=True)

    k_f = jnp.float32(topk)
    q_lane = lax.broadcasted_iota(jnp.int32, (1, tq), 1)
    n_adm = ((i * (tq // CHUNK) + q_lane // CHUNK + 1) * CHUNK).astype(F32)
    searched = n_adm > k_f
    c0 = count_if(lambda s, _: s >= 0.0)
    pos = c0 >= k_f
    thr0 = jnp.where(pos, jnp.int32(0), jnp.int32(-2 ** 31))
    cnt0 = jnp.where(pos, c0, (n_kb * kb_sz).astype(F32))

    def unresolved(cnt_thr):
        return jnp.max(jnp.where(searched, cnt_thr, k_f)) > k_f

    def bisect_cond(st):
        b, _, cnt_thr = st
        return jnp.logical_and(b < 31, unresolved(cnt_thr))

    def bisect(st):
        b, thr, cnt_thr = st
        cand = thr + (jnp.int32(1) << (30 - b))
        cand_f = _key_to_f32(cand)
        c = count_if(lambda s, _: s >= cand_f)
        ok = c >= k_f
        return b + 1, jnp.where(ok, cand, thr), jnp.where(ok, c, cnt_thr)

    _, thr, cnt_thr = lax.while_loop(bisect_cond, bisect, (jnp.int32(0), thr0, cnt0))
    thr_f = jnp.where(searched, _key_to_f32(thr), -jnp.inf)

    def tie_cut(_):
        need = k_f - count_if(lambda s, _: s > thr_f)
        n_bits = (sc_ref.shape[0] * kb_sz - 1).bit_length()

        def body(b, m):
            step = jnp.int32(1) << (n_bits - 1 - b)
            top = m + step - 1
            c = count_if(lambda s, k0: jnp.logical_and(s == thr_f, key_row + k0 <= top))
            return jnp.where(c < need, m + step, m)

        return lax.fori_loop(0, n_bits, body, jnp.zeros((1, tq), jnp.int32))

    idx_cut = lax.cond(unresolved(cnt_thr), tie_cut,
                       lambda _: jnp.full((1, tq), 2 ** 30, jnp.int32), 0)

    def bias_block(kb, _):
        s = sc_ref[kb]
        sel = jnp.logical_or(s > thr_f, jnp.logical_and(s == thr_f, key_row + kb * kb_sz <= idx_cut))
        sel = jnp.logical_and(sel, jnp.logical_or(kb < i, diag_adm))
        sc_ref[kb] = jnp.where(sel, 0.0, -jnp.inf)
        return 0

    lax.fori_loop(0, n_kb, bias_block, 0)

    m_ref[...] = jnp.full(m_ref.shape, NEG_BIG, F32)
    l_ref[...] = jnp.zeros(l_ref.shape, F32)
    acc_ref[...] = jnp.zeros(acc_ref.shape, F32)

    def att_block(kb, _):
        k0 = pl.multiple_of(kb * kb_sz, kb_sz)
        bias = sc_ref[kb]
        for h in range(N_HEADS):
            hs = slice(h * HEAD_DIM, (h + 1) * HEAD_DIM)
            s_ref[h] = lax.dot_general(k_ref[pl.ds(k0, kb_sz), hs], q_ref[:, hs],
                                       (((1,), (1,)), ((), ())),
                                       preferred_element_type=F32) + bias
        for h in range(N_HEADS):
            s = s_ref[h]
            m_old = m_ref[h]
            m_new = jnp.maximum(m_old, jnp.max(s, axis=0, keepdims=True))
            p = jnp.exp(s - m_new)
            alpha = jnp.exp(m_old - m_new)
            l_ref[h] = alpha * l_ref[h] + jnp.sum(p, axis=0, keepdims=True)
            p_ref[h] = p.astype(BF16)
            a_ref[h] = alpha
            m_ref[h] = m_new
        for h in range(N_HEADS):
            hs = slice(h * HEAD_DIM, (h + 1) * HEAD_DIM)
            acc_ref[h] = a_ref[h] * acc_ref[h] + jnp.dot(vt_ref[kb, hs, :], p_ref[h],
                                                        preferred_element_type=F32)
        return 0

    lax.fori_loop(0, n_kb, att_block, 0)

    ssq = jnp.zeros((1, tq), F32)
    for h in range(N_HEADS):
        yh = acc_ref[h] / l_ref[h]
        acc_ref[h] = yh
        ssq = ssq + jnp.sum(yh * yh, axis=0, keepdims=True)
    scale = lax.rsqrt(ssq / ATTN_DIM + RMS_EPS)
    for h in range(N_HEADS):
        hs = slice(h * HEAD_DIM, (h + 1) * HEAD_DIM)
        o_ref[:, hs] = ((acc_ref[h] * scale).T * gn_ref[:, hs]).astype(BF16)


def attn_group(q, k, v, qi, kk, wi, gn, nb, seq):
    tq = ATT_TQ
    tiles = seq // tq
    topk = min(TOPK_MAX, seq // 4)
    assert topk <= ATT_KB and ATT_TQ == ATT_KB and ATT_KB % CHUNK == 0
    qrow = lambda w: pl.BlockSpec((tq, w), lambda b, i: (b * tiles + i, 0))
    whole = lambda w: pl.BlockSpec((seq, w), lambda b, i: (b, 0), pipeline_mode=pl.Buffered(1))
    vt = jnp.swapaxes(v.reshape(nb, seq // ATT_KB, ATT_KB, ATTN_DIM), 2, 3)
    vt_spec = pl.BlockSpec((None, seq // ATT_KB, ATTN_DIM, ATT_KB), lambda b, i: (b, 0, 0, 0),
                           pipeline_mode=pl.Buffered(1))
    return pl.pallas_call(
        functools.partial(_attn_kernel, topk=topk),
        grid=(nb, tiles),
        in_specs=[qrow(ATTN_DIM), qrow(IDX_HEADS * IDX_DIM), qrow(LANES),
                  whole(ATTN_DIM), vt_spec, whole(LANES),
                  pl.BlockSpec((1, ATTN_DIM), lambda b, i: (0, 0))],
        out_specs=qrow(ATTN_DIM),
        out_shape=jax.ShapeDtypeStruct((nb * seq, ATTN_DIM), BF16),
        scratch_shapes=[pltpu.VMEM((seq // ATT_KB, ATT_KB, tq), F32),
                        pltpu.VMEM((IDX_HEADS, tq, LANES), BF16),
                        pltpu.VMEM((IDX_HEADS, tq, LANES), F32),
                        pltpu.VMEM((N_HEADS, 1, tq), F32),
                        pltpu.VMEM((N_HEADS, 1, tq), F32),
                        pltpu.VMEM((N_HEADS, 1, tq), F32),
                        pltpu.VMEM((N_HEADS, HEAD_DIM, tq), F32),
                        pltpu.VMEM((N_HEADS, ATT_KB, tq), F32),
                        pltpu.VMEM((N_HEADS, ATT_KB, tq), BF16)],
        compiler_params=_cparams(("arbitrary", "arbitrary")),
        name="attn",
    )(q, qi, wi, k, vt, kk, gn.reshape(1, -1))


def _layer_norm(r, g, b):
    mu = jnp.mean(r, axis=-1, keepdims=True)
    d = r - mu
    var = jnp.mean(d * d, axis=-1, keepdims=True)
    return d * lax.rsqrt(var + LN_EPS) * g + b


def _out_proj_kernel(x_ref, ys_ref, yc_ref, ya_ref, w_ref, gate_ref, g_ref, b_ref, o_ref):
    mix = jnp.dot(ys_ref[...], w_ref[pl.ds(0, SSM_DIM), :], preferred_element_type=F32)
    mix += jnp.dot(yc_ref[...], w_ref[pl.ds(SSM_DIM, CONV_DIM), :], preferred_element_type=F32)
    mix += jnp.dot(ya_ref[...], w_ref[pl.ds(SSM_DIM + CONV_DIM, ATTN_DIM), :],
                   preferred_element_type=F32)
    r = ALPHA * x_ref[...] + gate_ref[0] * mix
    o_ref[...] = _layer_norm(r, g_ref[...], b_ref[...])


def out_proj(x2, ys, yc, ya, w_o, gate, ln_g, ln_b, seq):
    n = x2.shape[0]
    tm = OUT_TM
    tiles = seq // tm
    row = lambda w: pl.BlockSpec((tm, w), lambda i: (i, 0))
    vec = pl.BlockSpec((1, D_MODEL), lambda i: (0, 0))
    return pl.pallas_call(
        _out_proj_kernel,
        grid=(n // tm,),
        in_specs=[row(D_MODEL), row(SSM_DIM), row(CONV_DIM), row(ATTN_DIM),
                  _resident((D_MODEL, D_MODEL)),
                  pl.BlockSpec((1, 1, D_MODEL), lambda i: (i // tiles, 0, 0)), vec, vec],
        out_specs=row(D_MODEL),
        out_shape=jax.ShapeDtypeStruct((n, D_MODEL), F32),
        compiler_params=_cparams(("arbitrary",)),
        name="out_proj",
    )(x2, ys, yc, ya, w_o.astype(BF16), gate, ln_g.reshape(1, -1), ln_b.reshape(1, -1))


def _ffn_kernel(x_ref, sc_ref, sh_ref, w1_ref, w2_ref, gate_ref, g_ref, b_ref, o_ref,
                h_ref, acc_ref):
    j = pl.program_id(1)

    @pl.when(j == 0)
    def _():
        h_ref[...] = (x_ref[...] * (1.0 + sc_ref[0]) + sh_ref[0]).astype(BF16)

    a = jnp.maximum(jnp.dot(h_ref[...], w1_ref[...], preferred_element_type=F32), 0.0)
    part = jnp.dot((a * a).astype(BF16), w2_ref[...], preferred_element_type=F32)

    @pl.when(j == 0)
    def _():
        acc_ref[...] = part

    @pl.when(j != 0)
    def _():
        acc_ref[...] += part

    @pl.when(j == pl.num_programs(1) - 1)
    def _():
        r = ALPHA * x_ref[...] + gate_ref[0] * acc_ref[...]
        o_ref[...] = _layer_norm(r, g_ref[...], b_ref[...])


def ffn(x2, sc, sh, w1, w2, gate, ln_g, ln_b, seq):
    n = x2.shape[0]
    tm, tf = FFN_TM, FFN_TF
    tiles = seq // tm
    row = pl.BlockSpec((tm, D_MODEL), lambda i, j: (i, 0))
    mod = pl.BlockSpec((1, 1, D_MODEL), lambda i, j: (i // tiles, 0, 0))
    vec = pl.BlockSpec((1, D_MODEL), lambda i, j: (0, 0))
    return pl.pallas_call(
        _ffn_kernel,
        grid=(n // tm, D_FF // tf),
        in_specs=[row, mod, mod,
                  pl.BlockSpec((D_MODEL, tf), lambda i, j: (0, j)),
                  pl.BlockSpec((tf, D_MODEL), lambda i, j: (j, 0)),
                  mod, vec, vec],
        out_specs=row,
        out_shape=jax.ShapeDtypeStruct((n, D_MODEL), F32),
        scratch_shapes=[pltpu.VMEM((tm, D_MODEL), BF16), pltpu.VMEM((tm, D_MODEL), F32)],
        compiler_params=_cparams(("arbitrary", "arbitrary")),
        name="ffn",
    )(x2, sc, sh, w1.astype(BF16), w2.astype(BF16), gate, ln_g.reshape(1, -1), ln_b.reshape(1, -1))


def kernel(x, c, w_ada, b_ada, w_in, lam_re, lam_im, log_dt, ssm_b_re, ssm_b_im, ssm_c_re, ssm_c_im,
           ssm_d, w_glu, b_glu, conv_w, gnorm_g, w_o, ln1_g, ln1_b, w_ff1, w_ff2, ln2_g, ln2_b):
    nb, seq, _ = x.shape
    tables = rope_tables(seq)
    mods = adaln(c, w_ada, b_ada)
    x2 = x.reshape(nb * seq, D_MODEL)
    for l in range(DEPTH):
        sh1, sc1, g1, sh2, sc2, g2 = (m[:, None, :] for m in jnp.split(mods[l], 6, axis=-1))
        u, conv_in, q, k, v, qi, kk, wi = in_proj(x2, sc1, sh1, w_in[l], tables, seq)
        gn = gnorm_g[l]
        ys = s5_group(u.reshape(nb, seq, SSM_DIM), lam_re[l], lam_im[l], log_dt[l], ssm_b_re[l],
                      ssm_b_im[l], ssm_c_re[l], ssm_c_im[l], ssm_d[l], w_glu[l], b_glu[l],
                      gn[:SSM_DIM]).reshape(nb * seq, SSM_DIM)
        yc = conv_group(conv_in, conv_w[l], gn[SSM_DIM:SSM_DIM + CONV_DIM], nb, seq)
        ya = attn_group(q, k, v, qi, kk, wi, gn[SSM_DIM + CONV_DIM:], nb, seq)
        x2 = out_proj(x2, ys, yc, ya, w_o[l], g1, ln1_g[l], ln1_b[l], seq)
        x2 = ffn(x2, sc2, sh2, w_ff1[l], w_ff2[l], g2, ln2_g[l], ln2_b[l], seq)
    return x2.reshape(nb, seq, D_MODEL)
```

```python
import functools
import math

import jax
import jax.numpy as jnp
from jax import lax
from jax.experimental import pallas as pl
from jax.experimental.pallas import tpu as pltpu

D_MODEL = 2048
DEPTH = 2
CHUNK = 64
SSM_DIM = 512
SSM_GROUP_CH = 16
SSM_GROUPS = 32
SSM_STATE = 64
CONV_DIM = 512
CONV_WIDTH = 3
ATTN_DIM = 1024
HEAD_DIM = 128
N_HEADS = 8
IDX_HEADS = 16
IDX_DIM = 64
TOPK_MAX = 256
D_FF = 4 * D_MODEL
ROPE_THETA = 10000.0
ALPHA = (2.0 * DEPTH) ** 0.25
LN_EPS = 1e-5
RMS_EPS = 1e-6

LANES = 128
SUBLANES = 8
VMEM_LIMIT = 56 * 1024 * 1024

ADA_TN = 1024
PROJ_TM = 256
S5_TC = 128
S5_HALF = SSM_DIM // 2
S5_NH = SSM_GROUPS // 2 * SSM_STATE
CONV_TM = 512
ATT_TQ = 256
ATT_KB = 256
OUT_TM = 512
FFN_TM = 512
FFN_TF = 1024
FFN_TN = 512

BF16 = jnp.bfloat16
F32 = jnp.float32
NEG_BIG = -1e30
Q_SCALE = HEAD_DIM ** -0.5 * math.log2(math.e)


def _cparams(sem):
    return pltpu.CompilerParams(dimension_semantics=sem, vmem_limit_bytes=VMEM_LIMIT)


def _resident(shape):
    nd = len(shape)
    return pl.BlockSpec(shape, lambda *_: (0,) * nd, pipeline_mode=pl.Buffered(1))


def _adaln_kernel(c_ref, w_ref, b_ref, o_ref):
    w = w_ref[0].astype(BF16)
    o_ref[0] = jnp.dot(c_ref[...], w, preferred_element_type=F32) + b_ref[0]


def adaln(c, w_ada, b_ada):
    nb = c.shape[0]
    rows = 16
    cp = jnp.zeros((rows, D_MODEL), BF16).at[:nb].set(c.astype(BF16))
    n_out = w_ada.shape[-1]
    out = pl.pallas_call(
        _adaln_kernel,
        grid=(DEPTH, n_out // ADA_TN),
        in_specs=[
            pl.BlockSpec((rows, D_MODEL), lambda l, j: (0, 0)),
            pl.BlockSpec((1, D_MODEL, ADA_TN), lambda l, j: (l, 0, j)),
            pl.BlockSpec((1, 1, ADA_TN), lambda l, j: (l, 0, j)),
        ],
        out_specs=pl.BlockSpec((1, rows, ADA_TN), lambda l, j: (l, 0, j)),
        out_shape=jax.ShapeDtypeStruct((DEPTH, rows, n_out), F32),
        compiler_params=_cparams(("arbitrary", "arbitrary")),
        name="adaln",
    )(cp, w_ada, b_ada.reshape(DEPTH, 1, n_out))
    return out[:, :nb]


def _rope_halves(x, cos, sin_signed, half):
    if 2 * half == LANES:
        swapped = pltpu.roll(x, half, axis=1)
    else:
        lane = lax.broadcasted_iota(jnp.int32, x.shape, 1)
        first = (lane % (2 * half)) < half
        swapped = jnp.where(first, pltpu.roll(x, LANES - half, axis=1), pltpu.roll(x, half, axis=1))
    return x * cos + swapped * sin_signed


def _in_proj_kernel(x_ref, sc_ref, sh_ref, wu_ref, wc_ref, wq_ref, wk_ref, wv_ref, wqi_ref, wt_ref,
                    cosa_ref, sina_ref, cosi_ref, sini_ref,
                    u_ref, conv_ref, q_ref, k_ref, vt_ref, qi_ref, kk_ref, wi_ref):
    h = (x_ref[...] * (1.0 + sc_ref[0]) + sh_ref[0]).astype(BF16)
    u_ref[...] = jnp.dot(h, wu_ref[...], preferred_element_type=F32)
    conv_ref[...] = jnp.dot(h, wc_ref[...], preferred_element_type=F32)
    vt_ref[...] = lax.dot_general(wv_ref[...], h, (((1,), (1,)), ((), ())),
                                  preferred_element_type=F32).astype(BF16)

    cosa, sina = cosa_ref[...], sina_ref[...]
    q = jnp.dot(h, wq_ref[...], preferred_element_type=F32)
    k = jnp.dot(h, wk_ref[...], preferred_element_type=F32)
    for hd in range(N_HEADS):
        sl = slice(hd * HEAD_DIM, (hd + 1) * HEAD_DIM)
        q_ref[:, sl] = (_rope_halves(q[:, sl], cosa, sina, HEAD_DIM // 2)
                        * Q_SCALE).astype(BF16)
        k_ref[:, sl] = _rope_halves(k[:, sl], cosa, sina, HEAD_DIM // 2).astype(BF16)

    cosi, sini = cosi_ref[...], sini_ref[...]
    qi = jnp.dot(h, wqi_ref[...], preferred_element_type=F32)
    for g in range(IDX_HEADS * IDX_DIM // LANES):
        sl = slice(g * LANES, (g + 1) * LANES)
        qi_ref[:, sl] = _rope_halves(qi[:, sl], cosi, sini, IDX_DIM // 2).astype(BF16)

    tail = jnp.dot(h, wt_ref[...], preferred_element_type=F32)
    ki2 = jnp.where(lax.broadcasted_iota(jnp.int32, tail.shape, 1) < IDX_DIM,
                    tail, pltpu.roll(tail, IDX_DIM, axis=1))
    kk_ref[...] = _rope_halves(ki2, cosi, sini, IDX_DIM // 2).astype(BF16)
    wi_ref[...] = pltpu.roll(tail, LANES - IDX_DIM, axis=1) * (
        (IDX_DIM ** -0.5) * (IDX_HEADS ** -0.5))


def in_proj(x, sc, sh, w_in, tables):
    nb, seq, _ = x.shape
    n = nb * seq
    tm = PROJ_TM
    assert tm == ATT_KB
    tiles_per_seq = seq // tm
    wb = w_in.astype(BF16)
    o = [0, SSM_DIM, SSM_DIM + 3 * CONV_DIM]
    o += [o[-1] + ATTN_DIM, o[-1] + 2 * ATTN_DIM, o[-1] + 3 * ATTN_DIM]
    o += [o[-1] + IDX_HEADS * IDX_DIM]
    wu, wc, wq, wk, wv, wqi = (wb[:, o[i]:o[i + 1]] for i in range(6))
    wv = wv.T
    tail_w = IDX_DIM + IDX_HEADS
    wt = jnp.zeros((D_MODEL, LANES), BF16).at[:, :tail_w].set(wb[:, o[6]:o[6] + tail_w])
    cosa, sina, cosi, sini = tables

    row = lambda w: pl.BlockSpec((tm, w), lambda i: (i, 0))
    mod = pl.BlockSpec((1, 1, D_MODEL), lambda i: (i // tiles_per_seq, 0, 0))
    tab = pl.BlockSpec((tm, LANES), lambda i: (i % tiles_per_seq, 0))
    outs = pl.pallas_call(
        _in_proj_kernel,
        grid=(n // tm,),
        in_specs=[pl.BlockSpec((None, tm, D_MODEL), lambda i: (i // tiles_per_seq, i % tiles_per_seq, 0)),
                  mod, mod,
                  _resident(wu.shape), _resident(wc.shape), _resident(wq.shape), _resident(wk.shape),
                  _resident(wv.shape), _resident(wqi.shape), _resident(wt.shape),
                  tab, tab, tab, tab],
        out_specs=[row(SSM_DIM), row(3 * CONV_DIM), row(ATTN_DIM), row(ATTN_DIM),
                   pl.BlockSpec((None, None, ATTN_DIM, tm),
                                lambda i: (i // tiles_per_seq, i % tiles_per_seq, 0, 0)),
                   row(IDX_HEADS * IDX_DIM), row(LANES), row(LANES)],
        out_shape=[jax.ShapeDtypeStruct((n, SSM_DIM), F32),
                   jax.ShapeDtypeStruct((n, 3 * CONV_DIM), F32),
                   jax.ShapeDtypeStruct((n, ATTN_DIM), BF16),
                   jax.ShapeDtypeStruct((n, ATTN_DIM), BF16),
                   jax.ShapeDtypeStruct((nb, tiles_per_seq, ATTN_DIM, tm), BF16),
                   jax.ShapeDtypeStruct((n, IDX_HEADS * IDX_DIM), BF16),
                   jax.ShapeDtypeStruct((n, LANES), BF16),
                   jax.ShapeDtypeStruct((n, LANES), F32)],
        compiler_params=_cparams(("arbitrary",)),
        name="in_proj",
    )(x, sc, sh, wu, wc, wq, wk, wv, wqi, wt, cosa, sina, cosi, sini)
    return outs


def rope_tables(seq):
    def tab(dim):
        inv = 1.0 / (ROPE_THETA ** (jnp.arange(0, dim, 2, dtype=F32) / dim))
        ang = jnp.arange(seq, dtype=F32)[:, None] * inv[None, :]
        cos, sin = jnp.cos(ang), jnp.sin(ang)
        reps = LANES // dim
        return (jnp.tile(jnp.concatenate([cos, cos], axis=1), (1, reps)),
                jnp.tile(jnp.concatenate([-sin, sin], axis=1), (1, reps)))
    cosa, sina = tab(HEAD_DIM)
    cosi, sini = tab(IDX_DIM)
    return cosa, sina, cosi, sini


def _s5_kernel(lhs_ref, utb_ref, bb_ref, cc_ref, are_ref, aim_ref, d_ref, wg_ref, bg_ref, gn_ref,
               o_ref, st_ref, xs_ref, y_ref, *, nb):
    rows_t = 2 * nb
    tc = xs_ref.shape[0] // rows_t

    @pl.when(pl.program_id(0) == 0)
    def _():
        st_ref[...] = jnp.zeros_like(st_ref)

    xs_ref[...] = jnp.dot(lhs_ref[...], bb_ref[...], preferred_element_type=F32)
    a_re, a_im = are_ref[...], aim_ref[...]

    def step(t, carry):
        xr, xi = carry
        r0 = pl.multiple_of(t * rows_t, rows_t)
        bur = xs_ref[pl.ds(r0, rows_t), :S5_NH]
        bui = xs_ref[pl.ds(r0, rows_t), S5_NH:]
        nr = a_re * xr - a_im * xi + bur
        ni = a_re * xi + a_im * xr + bui
        xs_ref[pl.ds(r0, rows_t), :S5_NH] = nr
        xs_ref[pl.ds(r0, rows_t), S5_NH:] = ni
        return nr, ni

    xr, xi = lax.fori_loop(0, tc, step, (st_ref[0], st_ref[1]), unroll=4)
    st_ref[0] = xr
    st_ref[1] = xi

    n_slab = SSM_DIM // LANES
    yy = jnp.dot(xs_ref[...].astype(BF16), cc_ref[...], preferred_element_type=F32)
    for s in range(n_slab):
        y_ref[s] = yy[:, s * LANES:(s + 1) * LANES]
    n_tb = tc * nb
    y = jnp.concatenate([y_ref[s, pl.ds(s // (n_slab // 2), n_tb, stride=2), :]
                         for s in range(n_slab)], axis=1)
    y = y + d_ref[...] * utb_ref[...]
    g = jax.nn.gelu(y)
    z = jnp.dot(g.astype(BF16), wg_ref[...], preferred_element_type=F32) + bg_ref[...]
    out = g * jax.nn.sigmoid(z)
    out = out * lax.rsqrt(jnp.mean(out * out, axis=-1, keepdims=True) + RMS_EPS) * gn_ref[...]
    for s in range(n_slab):
        y_ref[s, pl.ds(0, n_tb), :] = out[:, s * LANES:(s + 1) * LANES]
    for b in range(nb):
        for s in range(n_slab):
            o_ref[b, :, s * LANES:(s + 1) * LANES] = (
                y_ref[s, pl.ds(b, tc, stride=nb), :].astype(BF16))


def s5_group(u, lam_re, lam_im, log_dt, b_re, b_im, c_re, c_im, d_skip, w_glu, b_glu, gn):
    nb, seq, _ = u.shape
    assert 2 * nb == SUBLANES
    tc = S5_TC
    dt = jnp.exp(log_dt)[:, None]
    mag = jnp.exp(lam_re * dt)
    ang = lam_im * dt
    lb_re, lb_im = mag * jnp.cos(ang), mag * jnp.sin(ang)
    den = lam_re * lam_re + lam_im * lam_im
    n_re, n_im = lb_re - 1.0, lb_im
    f_re = (n_re * lam_re + n_im * lam_im) / den
    f_im = (n_im * lam_re - n_re * lam_im) / den
    bb_re = f_re[..., None] * b_re - f_im[..., None] * b_im
    bb_im = f_re[..., None] * b_im + f_im[..., None] * b_re
    gh = SSM_GROUPS // 2
    eye = jnp.eye(gh, dtype=F32)

    def in_mat(m):
        m = m.reshape(2, gh, SSM_STATE, SSM_GROUP_CH)
        bd = jnp.einsum('rgph,gk->rghkp', m, eye)
        return bd.reshape(SSM_DIM, S5_NH)

    def out_mat(m):
        m = m.reshape(2, gh, SSM_GROUP_CH, SSM_STATE)
        bd = jnp.einsum('rghp,gk->kprgh', m, eye)
        return bd.reshape(S5_NH, SSM_DIM)

    bb = jnp.concatenate([in_mat(bb_re), in_mat(bb_im)], axis=1).astype(BF16)
    cc = jnp.concatenate([out_mat(c_re), -out_mat(c_im)], axis=0).astype(BF16)

    def lane_vec(m):
        return jnp.tile(m.reshape(2, S5_NH), (nb, 1))

    a_re, a_im = lane_vec(lb_re), lane_vec(lb_im)

    utb = jnp.swapaxes(u, 0, 1)
    half_mask = (jnp.arange(SSM_DIM)[None, :] // S5_HALF) == jnp.arange(2)[:, None]
    lhs = jnp.where(half_mask[None, None], utb[:, :, None, :], 0.0).astype(BF16)
    lhs = lhs.reshape(seq * 2 * nb, SSM_DIM)
    utb = utb.reshape(seq * nb, SSM_DIM)

    rows = tc * 2 * nb
    out = pl.pallas_call(
        functools.partial(_s5_kernel, nb=nb),
        grid=(seq // tc,),
        in_specs=[
            pl.BlockSpec((rows, SSM_DIM), lambda i: (i, 0)),
            pl.BlockSpec((tc * nb, SSM_DIM), lambda i: (i, 0)),
            _resident(bb.shape), _resident(cc.shape),
            _resident(a_re.shape), _resident(a_im.shape),
            _resident((1, SSM_DIM)), _resident((SSM_DIM, SSM_DIM)),
            _resident((1, SSM_DIM)), _resident((1, SSM_DIM)),
        ],
        out_specs=pl.BlockSpec((nb, tc, SSM_DIM), lambda i: (0, i, 0)),
        out_shape=jax.ShapeDtypeStruct((nb, seq, SSM_DIM), BF16),
        scratch_shapes=[pltpu.VMEM((2, 2 * nb, S5_NH), F32),
                        pltpu.VMEM((rows, 2 * S5_NH), F32),
                        pltpu.VMEM((SSM_DIM // LANES, rows, LANES), F32)],
        compiler_params=_cparams(("arbitrary",)),
        name="s5",
    )(lhs, utb, bb, cc, a_re, a_im, d_skip.reshape(1, -1), w_glu.astype(BF16),
      b_glu.reshape(1, -1), gn.reshape(1, -1))
    return out


def _conv_kernel(c_ref, w_ref, gn_ref, o_ref, zp_ref):
    tm = o_ref.shape[0]
    ch = c_ref[:, :CONV_DIM]
    gb = c_ref[:, CONV_DIM:2 * CONV_DIM]
    gc = c_ref[:, 2 * CONV_DIM:]

    @pl.when(pl.program_id(1) == 0)
    def _():
        zp_ref[pl.ds(0, SUBLANES), :] = jnp.zeros((SUBLANES, CONV_DIM), F32)

    @pl.when(pl.program_id(1) != 0)
    def _():
        zp_ref[pl.ds(0, SUBLANES), :] = zp_ref[pl.ds(tm, SUBLANES), :]

    zp_ref[pl.ds(SUBLANES, tm), :] = gc * ch
    acc = zp_ref[pl.ds(SUBLANES, tm), :] * w_ref[2:3, :]
    acc += zp_ref[pl.ds(SUBLANES - 1, tm), :] * w_ref[1:2, :]
    acc += zp_ref[pl.ds(SUBLANES - 2, tm), :] * w_ref[0:1, :]
    y = gb * acc
    y = y * lax.rsqrt(jnp.mean(y * y, axis=-1, keepdims=True) + RMS_EPS) * gn_ref[...]
    o_ref[...] = y.astype(BF16)


def conv_group(conv_in, conv_w, gn, nb, seq):
    tm = CONV_TM
    tiles = seq // tm
    return pl.pallas_call(
        _conv_kernel,
        grid=(nb, tiles),
        in_specs=[pl.BlockSpec((tm, 3 * CONV_DIM), lambda b, i: (b * tiles + i, 0)),
                  pl.BlockSpec((CONV_WIDTH, CONV_DIM), lambda b, i: (0, 0)),
                  pl.BlockSpec((1, CONV_DIM), lambda b, i: (0, 0))],
        out_specs=pl.BlockSpec((tm, CONV_DIM), lambda b, i: (b * tiles + i, 0)),
        out_shape=jax.ShapeDtypeStruct((nb * seq, CONV_DIM), BF16),
        scratch_shapes=[pltpu.VMEM((tm + SUBLANES, CONV_DIM), F32)],
        compiler_params=_cparams(("arbitrary", "arbitrary")),
        name="conv",
    )(conv_in, conv_w, gn.reshape(1, -1))


def _key_to_f32(key):
    bits = key ^ ((key >> 31) & jnp.int32(0x7FFFFFFF))
    return pltpu.bitcast(bits, F32)


def _attn_kernel(q_ref, qi_ref, wi_ref, k_ref, vt_ref, kk_ref, gn_ref, o_ref,
                 sc_ref, lhs_ref, wib_ref, m_ref, l_ref, a_ref, acc_ref, s_ref, p_ref, *, topk):
    tq, kb_sz = ATT_TQ, ATT_KB
    i = pl.program_id(1)
    n_kb = i + 1
    lane128 = lax.broadcasted_iota(jnp.int32, (tq, LANES), 1)

    for h in range(IDX_HEADS):
        pair = qi_ref[:, (h // 2) * LANES:(h // 2 + 1) * LANES]
        mine = (lane128 // IDX_DIM) == (h % 2)
        lhs_ref[h] = jnp.where(mine, pair, jnp.zeros_like(pair))
        wib_ref[h] = jnp.broadcast_to(wi_ref[:, h:h + 1], (tq, LANES))

    key_row = lax.broadcasted_iota(jnp.int32, (kb_sz, tq), 0)
    qry_lane = lax.broadcasted_iota(jnp.int32, (kb_sz, tq), 1)
    diag_adm = key_row // CHUNK <= qry_lane // CHUNK

    def score_block(kb, _):
        k0 = pl.multiple_of(kb * kb_sz, kb_sz)
        kk = kk_ref[pl.ds(k0, kb_sz), :]
        acc = jnp.zeros((tq, kb_sz), F32)
        for h in range(IDX_HEADS):
            logit = lax.dot_general(lhs_ref[h], kk, (((1,), (1,)), ((), ())),
                                    preferred_element_type=F32)
            w = wib_ref[h]
            acc = acc + jnp.maximum(logit, 0.0) * jnp.concatenate([w] * (kb_sz // LANES), axis=1)
        sc_ref[kb] = jnp.where(jnp.logical_or(kb < i, diag_adm), acc.T, -jnp.inf)
        return 0

    lax.fori_loop(0, n_kb, score_block, 0)

    def count_if(pred):
        def one(kb, cnt):
            hit = jnp.where(pred(sc_ref[kb], kb * kb_sz), 1.0, 0.0)
            return cnt + jnp.sum(hit.reshape(kb_sz // SUBLANES, SUBLANES, tq), axis=0)

        def pair(j, cnt):
            return one(2 * j + 1, one(2 * j, cnt))

        cnt = lax.fori_loop(0, n_kb // 2, pair, jnp.zeros((SUBLANES, tq), F32))
        cnt = lax.cond(n_kb % 2 == 1, lambda c: one(n_kb - 1, c), lambda c: c, cnt)
        return jnp.sum(cnt, axis=0, keepdims=True)

    k_f = jnp.float32(topk)
    q_lane = lax.broadcasted_iota(jnp.int32, (1, tq), 1)
    n_adm = ((i * (tq // CHUNK) + q_lane // CHUNK + 1) * CHUNK).astype(F32)
    searched = n_adm > k_f
    c0 = count_if(lambda s, _: s >= 0.0)
    pos = c0 >= k_f
    thr0 = jnp.where(pos, jnp.int32(0), jnp.int32(-2 ** 31))
    cnt0 = jnp.where(pos, c0, (n_kb * kb_sz).astype(F32))

    def unresolved(cnt_thr):
        return jnp.max(jnp.where(searched, cnt_thr, k_f)) > k_f

    def bisect_cond(st):
        b, _, cnt_thr = st
        return jnp.logical_and(b < 31, unresolved(cnt_thr))

    def bisect(st):
        b, thr, cnt_thr = st
        cand = thr + (jnp.int32(1) << (30 - b))
        cand_f = _key_to_f32(cand)
        c = count_if(lambda s, _: s >= cand_f)
        ok = c >= k_f
        return b + 1, jnp.where(ok, cand, thr), jnp.where(ok, c, cnt_thr)

    _, thr, cnt_thr = lax.while_loop(bisect_cond, bisect, (jnp.int32(0), thr0, cnt0))
    thr_f = jnp.where(searched, _key_to_f32(thr), -jnp.inf)

    def tie_cut(_):
        need = k_f - count_if(lambda s, _: s > thr_f)
        n_bits = (sc_ref.shape[0] * kb_sz - 1).bit_length()

        def body(b, m):
            step = jnp.int32(1) << (n_bits - 1 - b)
            top = m + step - 1
            c = count_if(lambda s, k0: jnp.logical_and(s == thr_f, key_row + k0 <= top))
            return jnp.where(c < need, m + step, m)

        return lax.fori_loop(0, n_bits, body, jnp.zeros((1, tq), jnp.int32))

    idx_cut = lax.cond(unresolved(cnt_thr), tie_cut,
                       lambda _: jnp.full((1, tq), 2 ** 30, jnp.int32), 0)

    def bias_block(kb, _):
        s = sc_ref[kb]
        sel = jnp.logical_or(s > thr_f, jnp.logical_and(s == thr_f, key_row + kb * kb_sz <= idx_cut))
        sel = jnp.logical_and(sel, jnp.logical_or(kb < i, diag_adm))
        sc_ref[kb] = jnp.where(sel, 0.0, -jnp.inf)
        return 0

    lax.fori_loop(0, n_kb, bias_block, 0)

    m_ref[...] = jnp.full(m_ref.shape, NEG_BIG, F32)
    l_ref[...] = jnp.zeros(l_ref.shape, F32)
    acc_ref[...] = jnp.zeros(acc_ref.shape, F32)

    def att_block(kb, _):
        k0 = pl.multiple_of(kb * kb_sz, kb_sz)
        bias = sc_ref[kb]
        for h in range(N_HEADS):
            hs = slice(h * HEAD_DIM, (h + 1) * HEAD_DIM)
            s_ref[h] = lax.dot_general(k_ref[pl.ds(k0, kb_sz), hs], q_ref[:, hs],
                                       (((1,), (1,)), ((), ())),
                                       preferred_element_type=F32) + bias
        for h in range(N_HEADS):
            s = s_ref[h]
            m_old = m_ref[h]
            m_new = jnp.maximum(m_old, jnp.max(s, axis=0, keepdims=True))
            p = jnp.exp2(s - m_new)
            alpha = jnp.exp2(m_old - m_new)
            l_ref[h] = alpha * l_ref[h] + jnp.sum(p, axis=0, keepdims=True)
            p_ref[h] = p.astype(BF16)
            a_ref[h] = alpha
            m_ref[h] = m_new
        for h in range(N_HEADS):
            hs = slice(h * HEAD_DIM, (h + 1) * HEAD_DIM)
            acc_ref[h] = a_ref[h] * acc_ref[h] + jnp.dot(vt_ref[kb, hs, :], p_ref[h],
                                                        preferred_element_type=F32)
        return 0

    lax.fori_loop(0, n_kb, att_block, 0)

    ssq = jnp.zeros((1, tq), F32)
    for h in range(N_HEADS):
        yh = acc_ref[h] / l_ref[h]
        acc_ref[h] = yh
        ssq = ssq + jnp.sum(yh * yh, axis=0, keepdims=True)
    scale = lax.rsqrt(ssq / ATTN_DIM + RMS_EPS)
    for h in range(N_HEADS):
        hs = slice(h * HEAD_DIM, (h + 1) * HEAD_DIM)
        o_ref[:, hs] = ((acc_ref[h] * scale).T * gn_ref[:, hs]).astype(BF16)


def attn_group(q, k, vt, qi, kk, wi, gn, nb, seq):
    tq = ATT_TQ
    tiles = seq // tq
    topk = min(TOPK_MAX, seq // 4)
    assert topk <= ATT_KB and ATT_TQ == ATT_KB and ATT_KB % CHUNK == 0
    qrow = lambda w: pl.BlockSpec((tq, w), lambda b, i: (b * tiles + i, 0))
    whole = lambda w: pl.BlockSpec((seq, w), lambda b, i: (b, 0), pipeline_mode=pl.Buffered(1))
    vt_spec = pl.BlockSpec((None, seq // ATT_KB, ATTN_DIM, ATT_KB), lambda b, i: (b, 0, 0, 0),
                           pipeline_mode=pl.Buffered(1))
    return pl.pallas_call(
        functools.partial(_attn_kernel, topk=topk),
        grid=(nb, tiles),
        in_specs=[qrow(ATTN_DIM), qrow(IDX_HEADS * IDX_DIM), qrow(LANES),
                  whole(ATTN_DIM), vt_spec, whole(LANES),
                  pl.BlockSpec((1, ATTN_DIM), lambda b, i: (0, 0))],
        out_specs=qrow(ATTN_DIM),
        out_shape=jax.ShapeDtypeStruct((nb * seq, ATTN_DIM), BF16),
        scratch_shapes=[pltpu.VMEM((seq // ATT_KB, ATT_KB, tq), F32),
                        pltpu.VMEM((IDX_HEADS, tq, LANES), BF16),
                        pltpu.VMEM((IDX_HEADS, tq, LANES), F32),
                        pltpu.VMEM((N_HEADS, 1, tq), F32),
                        pltpu.VMEM((N_HEADS, 1, tq), F32),
                        pltpu.VMEM((N_HEADS, 1, tq), F32),
                        pltpu.VMEM((N_HEADS, HEAD_DIM, tq), F32),
                        pltpu.VMEM((N_HEADS, ATT_KB, tq), F32),
                        pltpu.VMEM((N_HEADS, ATT_KB, tq), BF16)],
        compiler_params=_cparams(("arbitrary", "arbitrary")),
        name="attn",
    )(q, qi, wi, k, vt, kk, gn.reshape(1, -1))


def _layer_norm(r, g, b):
    mu = jnp.mean(r, axis=-1, keepdims=True)
    d = r - mu
    var = jnp.mean(d * d, axis=-1, keepdims=True)
    return d * lax.rsqrt(var + LN_EPS) * g + b


def _out_proj_kernel(x_ref, ys_ref, yc_ref, ya_ref, w_ref, gate_ref, g_ref, b_ref, o_ref):
    mix = jnp.dot(ys_ref[...], w_ref[pl.ds(0, SSM_DIM), :], preferred_element_type=F32)
    mix += jnp.dot(yc_ref[...], w_ref[pl.ds(SSM_DIM, CONV_DIM), :], preferred_element_type=F32)
    mix += jnp.dot(ya_ref[...], w_ref[pl.ds(SSM_DIM + CONV_DIM, ATTN_DIM), :],
                   preferred_element_type=F32)
    r = ALPHA * x_ref[...] + gate_ref[0] * mix
    o_ref[...] = _layer_norm(r, g_ref[...], b_ref[...])


def out_proj(x, ys, yc, ya, w_o, gate, ln_g, ln_b):
    nb, seq, _ = x.shape
    tm = OUT_TM
    tiles = seq // tm
    row = lambda w: pl.BlockSpec((tm, w), lambda i: (i, 0))
    xrow = pl.BlockSpec((None, tm, D_MODEL), lambda i: (i // tiles, i % tiles, 0))
    vec = pl.BlockSpec((1, D_MODEL), lambda i: (0, 0))
    return pl.pallas_call(
        _out_proj_kernel,
        grid=(nb * tiles,),
        in_specs=[xrow, row(SSM_DIM), row(CONV_DIM), row(ATTN_DIM),
                  _resident((D_MODEL, D_MODEL)),
                  pl.BlockSpec((1, 1, D_MODEL), lambda i: (i // tiles, 0, 0)), vec, vec],
        out_specs=xrow,
        out_shape=jax.ShapeDtypeStruct((nb, seq, D_MODEL), F32),
        compiler_params=_cparams(("arbitrary",)),
        name="out_proj",
    )(x, ys, yc, ya, w_o.astype(BF16), gate, ln_g.reshape(1, -1), ln_b.reshape(1, -1))


def _ffn_kernel(x_ref, sc_ref, sh_ref, w1_ref, w2_ref, gate_ref, g_ref, b_ref, o_ref,
                h_ref, acc_ref):
    j = pl.program_id(1)

    @pl.when(j == 0)
    def _():
        h_ref[...] = (x_ref[...] * (1.0 + sc_ref[0]) + sh_ref[0]).astype(BF16)
        acc_ref[...] = jnp.zeros_like(acc_ref)

    a = jnp.maximum(jnp.dot(h_ref[...], w1_ref[...], preferred_element_type=F32), 0.0)
    a = (a * a).astype(BF16)
    for c in range(D_MODEL // FFN_TN):
        cs = slice(c * FFN_TN, (c + 1) * FFN_TN)
        acc_ref[:, cs] += jnp.dot(a, w2_ref[:, cs], preferred_element_type=F32)

    @pl.when(j == pl.num_programs(1) - 1)
    def _():
        r = ALPHA * x_ref[...] + gate_ref[0] * acc_ref[...]
        o_ref[...] = _layer_norm(r, g_ref[...], b_ref[...])


def ffn(x, sc, sh, w1, w2, gate, ln_g, ln_b):
    nb, seq, _ = x.shape
    tm, tf = FFN_TM, FFN_TF
    tiles = seq // tm
    row = pl.BlockSpec((None, tm, D_MODEL), lambda i, j: (i // tiles, i % tiles, 0))
    mod = pl.BlockSpec((1, 1, D_MODEL), lambda i, j: (i // tiles, 0, 0))
    vec = pl.BlockSpec((1, D_MODEL), lambda i, j: (0, 0))
    return pl.pallas_call(
        _ffn_kernel,
        grid=(nb * tiles, D_FF // tf),
        in_specs=[row, mod, mod,
                  pl.BlockSpec((D_MODEL, tf), lambda i, j: (0, j)),
                  pl.BlockSpec((tf, D_MODEL), lambda i, j: (j, 0)),
                  mod, vec, vec],
        out_specs=row,
        out_shape=jax.ShapeDtypeStruct((nb, seq, D_MODEL), F32),
        scratch_shapes=[pltpu.VMEM((tm, D_MODEL), BF16), pltpu.VMEM((tm, D_MODEL), F32)],
        compiler_params=_cparams(("arbitrary", "arbitrary")),
        name="ffn",
    )(x, sc, sh, w1.astype(BF16), w2.astype(BF16), gate, ln_g.reshape(1, -1), ln_b.reshape(1, -1))


def kernel(x, c, w_ada, b_ada, w_in, lam_re, lam_im, log_dt, ssm_b_re, ssm_b_im, ssm_c_re, ssm_c_im,
           ssm_d, w_glu, b_glu, conv_w, gnorm_g, w_o, ln1_g, ln1_b, w_ff1, w_ff2, ln2_g, ln2_b):
    nb, seq, _ = x.shape
    tables = rope_tables(seq)
    mods = adaln(c, w_ada, b_ada)
    for l in range(DEPTH):
        sh1, sc1, g1, sh2, sc2, g2 = (m[:, None, :] for m in jnp.split(mods[l], 6, axis=-1))
        u, conv_in, q, k, vt, qi, kk, wi = in_proj(x, sc1, sh1, w_in[l], tables)
        gn = gnorm_g[l]
        ys = s5_group(u.reshape(nb, seq, SSM_DIM), lam_re[l], lam_im[l], log_dt[l], ssm_b_re[l],
                      ssm_b_im[l], ssm_c_re[l], ssm_c_im[l], ssm_d[l], w_glu[l], b_glu[l],
                      gn[:SSM_DIM]).reshape(nb * seq, SSM_DIM)
        yc = conv_group(conv_in, conv_w[l], gn[SSM_DIM:SSM_DIM + CONV_DIM], nb, seq)
        ya = attn_group(q, k, vt, qi, kk, wi, gn[SSM_DIM + CONV_DIM:], nb, seq)
        x = out_proj(x, ys, yc, ya, w_o[l], g1, ln1_g[l], ln1_b[l])
        x = ffn(x, sc2, sh2, w_ff1[l], w_ff2[l], g2, ln2_g[l], ln2_b[l])
    return x
```

```python
import functools
import math

import jax
import jax.numpy as jnp
from jax import lax
from jax.experimental import pallas as pl
from jax.experimental.pallas import tpu as pltpu

D_MODEL = 2048
DEPTH = 2
CHUNK = 64
SSM_DIM = 512
SSM_GROUP_CH = 16
SSM_GROUPS = 32
SSM_STATE = 64
CONV_DIM = 512
CONV_WIDTH = 3
ATTN_DIM = 1024
HEAD_DIM = 128
N_HEADS = 8
IDX_HEADS = 16
IDX_DIM = 64
TOPK_MAX = 256
D_FF = 4 * D_MODEL
ROPE_THETA = 10000.0
ALPHA = (2.0 * DEPTH) ** 0.25
LN_EPS = 1e-5
RMS_EPS = 1e-6

_in_edges = [0, SSM_DIM, SSM_DIM + 3 * CONV_DIM]
_in_edges += [_in_edges[-1] + ATTN_DIM, _in_edges[-1] + 2 * ATTN_DIM, _in_edges[-1] + 3 * ATTN_DIM]
_in_edges += [_in_edges[-1] + IDX_HEADS * IDX_DIM]
_in_edges += [_in_edges[-1] + IDX_DIM + IDX_HEADS]
IN_COLS = {name: (_in_edges[j], _in_edges[j + 1])
           for j, name in enumerate(("u", "conv", "q", "k", "v", "qi", "tail"))}

LANES = 128
SUBLANES = 8
VMEM_LIMIT = 56 * 1024 * 1024

ADA_TN = 1024
PROJ_TM = 256
S5_TC = 128
S5_HALF = SSM_DIM // 2
S5_NH = SSM_GROUPS // 2 * SSM_STATE
CONV_TM = 512
ATT_TQ = 512
ATT_KB = 256
OUT_TM = 512
FFN_TM = 512
FFN_TF = 1024
FFN_TN = 512

BF16 = jnp.bfloat16
F32 = jnp.float32
NEG_BIG = -1e30
Q_SCALE = HEAD_DIM ** -0.5 * math.log2(math.e)


def _cparams(sem):
    return pltpu.CompilerParams(dimension_semantics=sem, vmem_limit_bytes=VMEM_LIMIT)


def _resident(shape):
    nd = len(shape)
    return pl.BlockSpec(shape, lambda *_: (0,) * nd, pipeline_mode=pl.Buffered(1))


def _adaln_kernel(c_ref, w_ref, b_ref, o_ref):
    w = w_ref[0].astype(BF16)
    o_ref[0] = jnp.dot(c_ref[...], w, preferred_element_type=F32) + b_ref[0]


def adaln(c, w_ada, b_ada):
    nb = c.shape[0]
    rows = 16
    cp = jnp.zeros((rows, D_MODEL), BF16).at[:nb].set(c.astype(BF16))
    n_out = w_ada.shape[-1]
    out = pl.pallas_call(
        _adaln_kernel,
        grid=(DEPTH, n_out // ADA_TN),
        in_specs=[
            pl.BlockSpec((rows, D_MODEL), lambda l, j: (0, 0)),
            pl.BlockSpec((1, D_MODEL, ADA_TN), lambda l, j: (l, 0, j)),
            pl.BlockSpec((1, 1, ADA_TN), lambda l, j: (l, 0, j)),
        ],
        out_specs=pl.BlockSpec((1, rows, ADA_TN), lambda l, j: (l, 0, j)),
        out_shape=jax.ShapeDtypeStruct((DEPTH, rows, n_out), F32),
        compiler_params=_cparams(("arbitrary", "arbitrary")),
        name="adaln",
    )(cp, w_ada, b_ada.reshape(DEPTH, 1, n_out))
    return out[:, :nb]


def _rope_halves(x, cos, sin_signed, half):
    if 2 * half == LANES:
        swapped = pltpu.roll(x, half, axis=1)
    else:
        lane = lax.broadcasted_iota(jnp.int32, x.shape, 1)
        first = (lane % (2 * half)) < half
        swapped = jnp.where(first, pltpu.roll(x, LANES - half, axis=1), pltpu.roll(x, half, axis=1))
    return x * cos + swapped * sin_signed


def _in_proj_kernel(x_ref, sc_ref, sh_ref, w_ref, wv_ref, wt_ref,
                    cosa_ref, sina_ref, cosi_ref, sini_ref,
                    u_ref, conv_ref, q_ref, k_ref, vt_ref, qi_ref, kk_ref, wi_ref):
    def proj(name):
        lo, hi = IN_COLS[name]
        return jnp.dot(h, w_ref[:, lo:hi], preferred_element_type=F32)

    h = (x_ref[...] * (1.0 + sc_ref[0]) + sh_ref[0]).astype(BF16)
    u_ref[...] = proj("u")
    conv_ref[...] = proj("conv")
    vt_ref[...] = lax.dot_general(wv_ref[...], h, (((1,), (1,)), ((), ())),
                                  preferred_element_type=F32).astype(BF16)

    cosa, sina = cosa_ref[...], sina_ref[...]
    q = proj("q")
    k = proj("k")
    for hd in range(N_HEADS):
        sl = slice(hd * HEAD_DIM, (hd + 1) * HEAD_DIM)
        q_ref[:, sl] = (_rope_halves(q[:, sl], cosa, sina, HEAD_DIM // 2)
                        * Q_SCALE).astype(BF16)
        k_ref[:, sl] = _rope_halves(k[:, sl], cosa, sina, HEAD_DIM // 2).astype(BF16)

    cosi, sini = cosi_ref[...], sini_ref[...]
    qi = proj("qi")
    for g in range(IDX_HEADS * IDX_DIM // LANES):
        sl = slice(g * LANES, (g + 1) * LANES)
        qi_ref[:, sl] = _rope_halves(qi[:, sl], cosi, sini, IDX_DIM // 2).astype(BF16)

    tail = jnp.dot(h, wt_ref[...], preferred_element_type=F32)
    ki2 = jnp.where(lax.broadcasted_iota(jnp.int32, tail.shape, 1) < IDX_DIM,
                    tail, pltpu.roll(tail, IDX_DIM, axis=1))
    kk_ref[...] = _rope_halves(ki2, cosi, sini, IDX_DIM // 2).astype(BF16)
    wi_ref[...] = pltpu.roll(tail, LANES - IDX_DIM, axis=1) * (
        (IDX_DIM ** -0.5) * (IDX_HEADS ** -0.5))


def in_proj(x, sc, sh, w_in_bf16, layer, tables):
    nb, seq, _ = x.shape
    n = nb * seq
    tm = PROJ_TM
    assert tm == ATT_KB
    tiles_per_seq = seq // tm
    lo, hi = IN_COLS["v"]
    wv = w_in_bf16[layer, :, lo:hi].T
    lo, hi = IN_COLS["tail"]
    wt = jnp.zeros((D_MODEL, LANES), BF16).at[:, :hi - lo].set(w_in_bf16[layer, :, lo:hi])
    main_cols = IN_COLS["tail"][0]
    cosa, sina, cosi, sini = tables

    row = lambda w: pl.BlockSpec((tm, w), lambda i: (i, 0))
    mod = pl.BlockSpec((1, 1, D_MODEL), lambda i: (i // tiles_per_seq, 0, 0))
    tab = pl.BlockSpec((tm, LANES), lambda i: (i % tiles_per_seq, 0))
    outs = pl.pallas_call(
        _in_proj_kernel,
        grid=(n // tm,),
        in_specs=[pl.BlockSpec((None, tm, D_MODEL), lambda i: (i // tiles_per_seq, i % tiles_per_seq, 0)),
                  mod, mod,
                  pl.BlockSpec((None, D_MODEL, main_cols), lambda i: (layer, 0, 0),
                               pipeline_mode=pl.Buffered(1)),
                  _resident(wv.shape), _resident(wt.shape),
                  tab, tab, tab, tab],
        out_specs=[row(SSM_DIM), row(3 * CONV_DIM), row(ATTN_DIM), row(ATTN_DIM),
                   pl.BlockSpec((None, None, ATTN_DIM, tm),
                                lambda i: (i // tiles_per_seq, i % tiles_per_seq, 0, 0)),
                   row(IDX_HEADS * IDX_DIM), row(LANES), row(LANES)],
        out_shape=[jax.ShapeDtypeStruct((n, SSM_DIM), F32),
                   jax.ShapeDtypeStruct((n, 3 * CONV_DIM), F32),
                   jax.ShapeDtypeStruct((n, ATTN_DIM), BF16),
                   jax.ShapeDtypeStruct((n, ATTN_DIM), BF16),
                   jax.ShapeDtypeStruct((nb, tiles_per_seq, ATTN_DIM, tm), BF16),
                   jax.ShapeDtypeStruct((n, IDX_HEADS * IDX_DIM), BF16),
                   jax.ShapeDtypeStruct((n, LANES), BF16),
                   jax.ShapeDtypeStruct((n, LANES), F32)],
        compiler_params=_cparams(("arbitrary",)),
        name="in_proj",
    )(x, sc, sh, w_in_bf16, wv, wt, cosa, sina, cosi, sini)
    return outs


def rope_tables(seq):
    def tab(dim):
        inv = 1.0 / (ROPE_THETA ** (jnp.arange(0, dim, 2, dtype=F32) / dim))
        ang = jnp.arange(seq, dtype=F32)[:, None] * inv[None, :]
        cos, sin = jnp.cos(ang), jnp.sin(ang)
        reps = LANES // dim
        return (jnp.tile(jnp.concatenate([cos, cos], axis=1), (1, reps)),
                jnp.tile(jnp.concatenate([-sin, sin], axis=1), (1, reps)))
    cosa, sina = tab(HEAD_DIM)
    cosi, sini = tab(IDX_DIM)
    return cosa, sina, cosi, sini


def _s5_kernel(utb_ref, bb_ref, cc_ref, are_ref, aim_ref, d_ref, wg_ref, bg_ref, gn_ref,
               o_ref, st_ref, xs_ref, y_ref, lhs_ref, *, nb):
    rows_t = 2 * nb
    tc = xs_ref.shape[0] // rows_t
    n_tb = tc * nb
    n_slab = SSM_DIM // LANES

    @pl.when(pl.program_id(0) == 0)
    def _():
        st_ref[...] = jnp.zeros_like(st_ref)
        lhs_ref[...] = jnp.zeros_like(lhs_ref)

    for s in range(n_slab):
        lhs_ref[s, pl.ds(s // (n_slab // 2), n_tb, stride=2), :] = utb_ref[:, s * LANES:(s + 1) * LANES]
    lhs = jnp.concatenate([lhs_ref[s] for s in range(n_slab)], axis=1).astype(BF16)
    xs_ref[...] = jnp.dot(lhs, bb_ref[...], preferred_element_type=F32)
    a_re, a_im = are_ref[...], aim_ref[...]

    def step(t, carry):
        xr, xi = carry
        r0 = pl.multiple_of(t * rows_t, rows_t)
        bur = xs_ref[pl.ds(r0, rows_t), :S5_NH]
        bui = xs_ref[pl.ds(r0, rows_t), S5_NH:]
        nr = a_re * xr - a_im * xi + bur
        ni = a_re * xi + a_im * xr + bui
        xs_ref[pl.ds(r0, rows_t), :S5_NH] = nr
        xs_ref[pl.ds(r0, rows_t), S5_NH:] = ni
        return nr, ni

    xr, xi = lax.fori_loop(0, tc, step, (st_ref[0], st_ref[1]), unroll=4)
    st_ref[0] = xr
    st_ref[1] = xi

    yy = jnp.dot(xs_ref[...].astype(BF16), cc_ref[...], preferred_element_type=F32)
    for s in range(n_slab):
        y_ref[s] = yy[:, s * LANES:(s + 1) * LANES]
    y = jnp.concatenate([y_ref[s, pl.ds(s // (n_slab // 2), n_tb, stride=2), :]
                         for s in range(n_slab)], axis=1)
    y = y + d_ref[...] * utb_ref[...]
    g = jax.nn.gelu(y)
    z = jnp.dot(g.astype(BF16), wg_ref[...], preferred_element_type=F32) + bg_ref[...]
    out = g * jax.nn.sigmoid(z)
    out = out * lax.rsqrt(jnp.mean(out * out, axis=-1, keepdims=True) + RMS_EPS) * gn_ref[...]
    for s in range(n_slab):
        y_ref[s, pl.ds(0, n_tb), :] = out[:, s * LANES:(s + 1) * LANES]
    for b in range(nb):
        for s in range(n_slab):
            o_ref[b, :, s * LANES:(s + 1) * LANES] = (
                y_ref[s, pl.ds(b, tc, stride=nb), :].astype(BF16))


def s5_group(u, lam_re, lam_im, log_dt, b_re, b_im, c_re, c_im, d_skip, w_glu, b_glu, gn):
    nb, seq, _ = u.shape
    assert 2 * nb == SUBLANES
    tc = S5_TC
    dt = jnp.exp(log_dt)[:, None]
    mag = jnp.exp(lam_re * dt)
    ang = lam_im * dt
    lb_re, lb_im = mag * jnp.cos(ang), mag * jnp.sin(ang)
    den = lam_re * lam_re + lam_im * lam_im
    n_re, n_im = lb_re - 1.0, lb_im
    f_re = (n_re * lam_re + n_im * lam_im) / den
    f_im = (n_im * lam_re - n_re * lam_im) / den
    bb_re = f_re[..., None] * b_re - f_im[..., None] * b_im
    bb_im = f_re[..., None] * b_im + f_im[..., None] * b_re
    gh = SSM_GROUPS // 2
    eye = jnp.eye(gh, dtype=F32)

    def in_mat(m):
        m = m.reshape(2, gh, SSM_STATE, SSM_GROUP_CH)
        bd = jnp.einsum('rgph,gk->rghkp', m, eye)
        return bd.reshape(SSM_DIM, S5_NH)

    def out_mat(m):
        m = m.reshape(2, gh, SSM_GROUP_CH, SSM_STATE)
        bd = jnp.einsum('rghp,gk->kprgh', m, eye)
        return bd.reshape(S5_NH, SSM_DIM)

    bb = jnp.concatenate([in_mat(bb_re), in_mat(bb_im)], axis=1).astype(BF16)
    cc = jnp.concatenate([out_mat(c_re), -out_mat(c_im)], axis=0).astype(BF16)

    def lane_vec(m):
        return jnp.tile(m.reshape(2, S5_NH), (nb, 1))

    a_re, a_im = lane_vec(lb_re), lane_vec(lb_im)

    utb = jnp.swapaxes(u, 0, 1).reshape(seq * nb, SSM_DIM)

    rows = tc * 2 * nb
    out = pl.pallas_call(
        functools.partial(_s5_kernel, nb=nb),
        grid=(seq // tc,),
        in_specs=[
            pl.BlockSpec((tc * nb, SSM_DIM), lambda i: (i, 0)),
            _resident(bb.shape), _resident(cc.shape),
            _resident(a_re.shape), _resident(a_im.shape),
            _resident((1, SSM_DIM)), _resident((SSM_DIM, SSM_DIM)),
            _resident((1, SSM_DIM)), _resident((1, SSM_DIM)),
        ],
        out_specs=pl.BlockSpec((nb, tc, SSM_DIM), lambda i: (0, i, 0)),
        out_shape=jax.ShapeDtypeStruct((nb, seq, SSM_DIM), BF16),
        scratch_shapes=[pltpu.VMEM((2, 2 * nb, S5_NH), F32),
                        pltpu.VMEM((rows, 2 * S5_NH), F32),
                        pltpu.VMEM((SSM_DIM // LANES, rows, LANES), F32),
                        pltpu.VMEM((SSM_DIM // LANES, rows, LANES), F32)],
        compiler_params=_cparams(("arbitrary",)),
        name="s5",
    )(utb, bb, cc, a_re, a_im, d_skip.reshape(1, -1), w_glu.astype(BF16),
      b_glu.reshape(1, -1), gn.reshape(1, -1))
    return out


def _conv_kernel(c_ref, w_ref, gn_ref, o_ref, zp_ref):
    tm = o_ref.shape[0]
    ch = c_ref[:, :CONV_DIM]
    gb = c_ref[:, CONV_DIM:2 * CONV_DIM]
    gc = c_ref[:, 2 * CONV_DIM:]

    @pl.when(pl.program_id(1) == 0)
    def _():
        zp_ref[pl.ds(0, SUBLANES), :] = jnp.zeros((SUBLANES, CONV_DIM), F32)

    @pl.when(pl.program_id(1) != 0)
    def _():
        zp_ref[pl.ds(0, SUBLANES), :] = zp_ref[pl.ds(tm, SUBLANES), :]

    zp_ref[pl.ds(SUBLANES, tm), :] = gc * ch
    acc = zp_ref[pl.ds(SUBLANES, tm), :] * w_ref[2:3, :]
    acc += zp_ref[pl.ds(SUBLANES - 1, tm), :] * w_ref[1:2, :]
    acc += zp_ref[pl.ds(SUBLANES - 2, tm), :] * w_ref[0:1, :]
    y = gb * acc
    y = y * lax.rsqrt(jnp.mean(y * y, axis=-1, keepdims=True) + RMS_EPS) * gn_ref[...]
    o_ref[...] = y.astype(BF16)


def conv_group(conv_in, conv_w, gn, nb, seq):
    tm = CONV_TM
    tiles = seq // tm
    return pl.pallas_call(
        _conv_kernel,
        grid=(nb, tiles),
        in_specs=[pl.BlockSpec((tm, 3 * CONV_DIM), lambda b, i: (b * tiles + i, 0)),
                  pl.BlockSpec((CONV_WIDTH, CONV_DIM), lambda b, i: (0, 0)),
                  pl.BlockSpec((1, CONV_DIM), lambda b, i: (0, 0))],
        out_specs=pl.BlockSpec((tm, CONV_DIM), lambda b, i: (b * tiles + i, 0)),
        out_shape=jax.ShapeDtypeStruct((nb * seq, CONV_DIM), BF16),
        scratch_shapes=[pltpu.VMEM((tm + SUBLANES, CONV_DIM), F32)],
        compiler_params=_cparams(("arbitrary", "arbitrary")),
        name="conv",
    )(conv_in, conv_w, gn.reshape(1, -1))


def _key_to_f32(key):
    bits = key ^ ((key >> 31) & jnp.int32(0x7FFFFFFF))
    return pltpu.bitcast(bits, F32)


def _attn_kernel(q_ref, qi_ref, wi_ref, k_ref, vt_ref, kk_ref, gn_ref, o_ref,
                 sc_ref, lhs_ref, wib_ref, m_ref, l_ref, a_ref, acc_ref, s_ref, p_ref, *, topk):
    tq, kb_sz = ATT_TQ, ATT_KB
    i = pl.program_id(1)
    n_kb = (i + 1) * (tq // kb_sz)
    lane128 = lax.broadcasted_iota(jnp.int32, (tq, LANES), 1)

    for h in range(IDX_HEADS):
        pair = qi_ref[:, (h // 2) * LANES:(h // 2 + 1) * LANES]
        mine = (lane128 // IDX_DIM) == (h % 2)
        lhs_ref[h] = jnp.where(mine, pair, jnp.zeros_like(pair))
        wib_ref[h] = jnp.broadcast_to(wi_ref[:, h:h + 1], (tq, LANES))

    key_row = lax.broadcasted_iota(jnp.int32, (kb_sz, tq), 0)
    qry_lane = lax.broadcasted_iota(jnp.int32, (kb_sz, tq), 1)
    key_chunk = key_row // CHUNK
    qry_chunk = qry_lane // CHUNK + i * (tq // CHUNK)

    def admissible(kb):
        return key_chunk + kb * (kb_sz // CHUNK) <= qry_chunk

    def score_block(kb, _):
        k0 = pl.multiple_of(kb * kb_sz, kb_sz)
        kk = kk_ref[pl.ds(k0, kb_sz), :]
        acc = jnp.zeros((tq, kb_sz), F32)
        for h in range(IDX_HEADS):
            logit = lax.dot_general(lhs_ref[h], kk, (((1,), (1,)), ((), ())),
                                    preferred_element_type=F32)
            w = wib_ref[h]
            acc = acc + jnp.maximum(logit, 0.0) * jnp.concatenate([w] * (kb_sz // LANES), axis=1)
        sc_ref[kb] = jnp.where(admissible(kb), acc.T, -jnp.inf)
        return 0

    lax.fori_loop(0, n_kb, score_block, 0)

    @pl.when(n_kb % 2 == 1)
    def _():
        sc_ref[n_kb] = jnp.full((kb_sz, tq), -jnp.inf, F32)

    def count_if(pred):
        def one(kb, cnt):
            hit = jnp.where(pred(sc_ref[kb], kb * kb_sz), 1.0, 0.0)
            return cnt + jnp.sum(hit.reshape(kb_sz // SUBLANES, SUBLANES, tq), axis=0)

        def pair(j, cnt):
            return one(2 * j + 1, one(2 * j, cnt))

        cnt = lax.fori_loop(0, (n_kb + 1) // 2, pair, jnp.zeros((SUBLANES, tq), F32))
        return jnp.sum(cnt, axis=0, keepdims=True)

    k_f = jnp.float32(topk)
    q_lane = lax.broadcasted_iota(jnp.int32, (1, tq), 1)
    n_adm = ((i * (tq // CHUNK) + q_lane // CHUNK + 1) * CHUNK).astype(F32)
    searched = n_adm > k_f
    c0 = count_if(lambda s, _: s >= 0.0)
    pos = c0 >= k_f
    thr0 = jnp.where(pos, jnp.int32(0), jnp.int32(-2 ** 31))
    cnt0 = jnp.where(pos, c0, (n_kb * kb_sz).astype(F32))

    def unresolved(cnt_thr):
        return jnp.max(jnp.where(searched, cnt_thr, k_f)) > k_f

    def bisect_cond(st):
        b, _, cnt_thr = st
        return jnp.logical_and(b < 31, unresolved(cnt_thr))

    def bisect(st):
        b, thr, cnt_thr = st
        cand = thr + (jnp.int32(1) << (30 - b))
        cand_f = _key_to_f32(cand)
        c = count_if(lambda s, _: s >= cand_f)
        ok = c >= k_f
        return b + 1, jnp.where(ok, cand, thr), jnp.where(ok, c, cnt_thr)

    _, thr, cnt_thr = lax.while_loop(bisect_cond, bisect, (jnp.int32(0), thr0, cnt0))
    thr_f = jnp.where(searched, _key_to_f32(thr), -jnp.inf)

    def tie_cut(_):
        need = k_f - count_if(lambda s, _: s > thr_f)
        n_bits = (sc_ref.shape[0] * kb_sz - 1).bit_length()

        def body(b, m):
            step = jnp.int32(1) << (n_bits - 1 - b)
            top = m + step - 1
            c = count_if(lambda s, k0: jnp.logical_and(s == thr_f, key_row + k0 <= top))
            return jnp.where(c < need, m + step, m)

        return lax.fori_loop(0, n_bits, body, jnp.zeros((1, tq), jnp.int32))

    idx_cut = lax.cond(unresolved(cnt_thr), tie_cut,
                       lambda _: jnp.full((1, tq), 2 ** 30, jnp.int32), 0)

    def bias_block(kb, _):
        s = sc_ref[kb]
        sel = jnp.logical_or(s > thr_f, jnp.logical_and(s == thr_f, key_row + kb * kb_sz <= idx_cut))
        sel = jnp.logical_and(sel, admissible(kb))
        sc_ref[kb] = jnp.where(sel, 0.0, -jnp.inf)
        return 0

    lax.fori_loop(0, n_kb, bias_block, 0)

    m_ref[...] = jnp.full(m_ref.shape, NEG_BIG, F32)
    l_ref[...] = jnp.zeros(l_ref.shape, F32)
    acc_ref[...] = jnp.zeros(acc_ref.shape, F32)

    def att_block(kb, _):
        k0 = pl.multiple_of(kb * kb_sz, kb_sz)
        bias = sc_ref[kb]
        for h in range(N_HEADS):
            hs = slice(h * HEAD_DIM, (h + 1) * HEAD_DIM)
            s_ref[h] = lax.dot_general(k_ref[pl.ds(k0, kb_sz), hs], q_ref[:, hs],
                                       (((1,), (1,)), ((), ())),
                                       preferred_element_type=F32) + bias
        for h in range(N_HEADS):
            s = s_ref[h]
            m_old = m_ref[h]
            m_new = jnp.maximum(m_old, jnp.max(s, axis=0, keepdims=True))
            p = jnp.exp2(s - m_new)
            alpha = jnp.exp2(m_old - m_new)
            l_ref[h] = alpha * l_ref[h] + jnp.sum(p, axis=0, keepdims=True)
            p_ref[h] = p.astype(BF16)
            a_ref[h] = alpha
            m_ref[h] = m_new
        for h in range(N_HEADS):
            hs = slice(h * HEAD_DIM, (h + 1) * HEAD_DIM)
            acc_ref[h] = a_ref[h] * acc_ref[h] + jnp.dot(vt_ref[kb, hs, :], p_ref[h],
                                                        preferred_element_type=F32)
        return 0

    lax.fori_loop(0, n_kb, att_block, 0)

    ssq = jnp.zeros((1, tq), F32)
    for h in range(N_HEADS):
        yh = acc_ref[h] / l_ref[h]
        acc_ref[h] = yh
        ssq = ssq + jnp.sum(yh * yh, axis=0, keepdims=True)
    scale = lax.rsqrt(ssq / ATTN_DIM + RMS_EPS)
    for h in range(N_HEADS):
        hs = slice(h * HEAD_DIM, (h + 1) * HEAD_DIM)
        o_ref[:, hs] = ((acc_ref[h] * scale).T * gn_ref[:, hs]).astype(BF16)


def attn_group(q, k, vt, qi, kk, wi, gn, nb, seq):
    tq = ATT_TQ
    tiles = seq // tq
    topk = min(TOPK_MAX, seq // 4)
    assert topk <= ATT_KB and ATT_TQ % ATT_KB == 0 and ATT_KB % CHUNK == 0 and (seq // ATT_KB) % 2 == 0
    qrow = lambda w: pl.BlockSpec((tq, w), lambda b, i: (b * tiles + i, 0))
    whole = lambda w: pl.BlockSpec((seq, w), lambda b, i: (b, 0), pipeline_mode=pl.Buffered(1))
    vt_spec = pl.BlockSpec((None, seq // ATT_KB, ATTN_DIM, ATT_KB), lambda b, i: (b, 0, 0, 0),
                           pipeline_mode=pl.Buffered(1))
    return pl.pallas_call(
        functools.partial(_attn_kernel, topk=topk),
        grid=(nb, tiles),
        in_specs=[qrow(ATTN_DIM), qrow(IDX_HEADS * IDX_DIM), qrow(LANES),
                  whole(ATTN_DIM), vt_spec, whole(LANES),
                  pl.BlockSpec((1, ATTN_DIM), lambda b, i: (0, 0))],
        out_specs=qrow(ATTN_DIM),
        out_shape=jax.ShapeDtypeStruct((nb * seq, ATTN_DIM), BF16),
        scratch_shapes=[pltpu.VMEM((seq // ATT_KB, ATT_KB, tq), F32),
                        pltpu.VMEM((IDX_HEADS, tq, LANES), BF16),
                        pltpu.VMEM((IDX_HEADS, tq, LANES), F32),
                        pltpu.VMEM((N_HEADS, 1, tq), F32),
                        pltpu.VMEM((N_HEADS, 1, tq), F32),
                        pltpu.VMEM((N_HEADS, 1, tq), F32),
                        pltpu.VMEM((N_HEADS, HEAD_DIM, tq), F32),
                        pltpu.VMEM((N_HEADS, ATT_KB, tq), F32),
                        pltpu.VMEM((N_HEADS, ATT_KB, tq), BF16)],
        compiler_params=_cparams(("arbitrary", "arbitrary")),
        name="attn",
    )(q, qi, wi, k, vt, kk, gn.reshape(1, -1))


def _layer_norm(r, g, b):
    mu = jnp.mean(r, axis=-1, keepdims=True)
    d = r - mu
    var = jnp.mean(d * d, axis=-1, keepdims=True)
    return d * lax.rsqrt(var + LN_EPS) * g + b


def _out_proj_kernel(x_ref, ys_ref, yc_ref, ya_ref, w_ref, gate_ref, g_ref, b_ref, o_ref):
    mix = jnp.dot(ys_ref[...], w_ref[pl.ds(0, SSM_DIM), :], preferred_element_type=F32)
    mix += jnp.dot(yc_ref[...], w_ref[pl.ds(SSM_DIM, CONV_DIM), :], preferred_element_type=F32)
    mix += jnp.dot(ya_ref[...], w_ref[pl.ds(SSM_DIM + CONV_DIM, ATTN_DIM), :],
                   preferred_element_type=F32)
    r = ALPHA * x_ref[...] + gate_ref[0] * mix
    o_ref[...] = _layer_norm(r, g_ref[...], b_ref[...])


def out_proj(x, ys, yc, ya, w_o, gate, ln_g, ln_b):
    nb, seq, _ = x.shape
    tm = OUT_TM
    tiles = seq // tm
    row = lambda w: pl.BlockSpec((tm, w), lambda i: (i, 0))
    xrow = pl.BlockSpec((None, tm, D_MODEL), lambda i: (i // tiles, i % tiles, 0))
    vec = pl.BlockSpec((1, D_MODEL), lambda i: (0, 0))
    return pl.pallas_call(
        _out_proj_kernel,
        grid=(nb * tiles,),
        in_specs=[xrow, row(SSM_DIM), row(CONV_DIM), row(ATTN_DIM),
                  _resident((D_MODEL, D_MODEL)),
                  pl.BlockSpec((1, 1, D_MODEL), lambda i: (i // tiles, 0, 0)), vec, vec],
        out_specs=xrow,
        out_shape=jax.ShapeDtypeStruct((nb, seq, D_MODEL), F32),
        compiler_params=_cparams(("arbitrary",)),
        name="out_proj",
    )(x, ys, yc, ya, w_o.astype(BF16), gate, ln_g.reshape(1, -1), ln_b.reshape(1, -1))


def _ffn_kernel(x_ref, sc_ref, sh_ref, w1_ref, w2_ref, gate_ref, g_ref, b_ref, o_ref,
                h_ref, acc_ref):
    j = pl.program_id(1)

    @pl.when(j == 0)
    def _():
        h_ref[...] = (x_ref[...] * (1.0 + sc_ref[0]) + sh_ref[0]).astype(BF16)
        acc_ref[...] = jnp.zeros_like(acc_ref)

    a = jnp.maximum(jnp.dot(h_ref[...], w1_ref[...], preferred_element_type=F32), 0.0)
    a = (a * a).astype(BF16)
    for c in range(D_MODEL // FFN_TN):
        cs = slice(c * FFN_TN, (c + 1) * FFN_TN)
        acc_ref[:, cs] += jnp.dot(a, w2_ref[:, cs], preferred_element_type=F32)

    @pl.when(j == pl.num_programs(1) - 1)
    def _():
        r = ALPHA * x_ref[...] + gate_ref[0] * acc_ref[...]
        o_ref[...] = _layer_norm(r, g_ref[...], b_ref[...])


def ffn(x, sc, sh, w1, w2, gate, ln_g, ln_b):
    nb, seq, _ = x.shape
    tm, tf = FFN_TM, FFN_TF
    tiles = seq // tm
    row = pl.BlockSpec((None, tm, D_MODEL), lambda i, j: (i // tiles, i % tiles, 0))
    mod = pl.BlockSpec((1, 1, D_MODEL), lambda i, j: (i // tiles, 0, 0))
    vec = pl.BlockSpec((1, D_MODEL), lambda i, j: (0, 0))
    return pl.pallas_call(
        _ffn_kernel,
        grid=(nb * tiles, D_FF // tf),
        in_specs=[row, mod, mod,
                  pl.BlockSpec((D_MODEL, tf), lambda i, j: (0, j)),
                  pl.BlockSpec((tf, D_MODEL), lambda i, j: (j, 0)),
                  mod, vec, vec],
        out_specs=row,
        out_shape=jax.ShapeDtypeStruct((nb, seq, D_MODEL), F32),
        scratch_shapes=[pltpu.VMEM((tm, D_MODEL), BF16), pltpu.VMEM((tm, D_MODEL), F32)],
        compiler_params=_cparams(("arbitrary", "arbitrary")),
        name="ffn",
    )(x, sc, sh, w1.astype(BF16), w2.astype(BF16), gate, ln_g.reshape(1, -1), ln_b.reshape(1, -1))


def kernel(x, c, w_ada, b_ada, w_in, lam_re, lam_im, log_dt, ssm_b_re, ssm_b_im, ssm_c_re, ssm_c_im,
           ssm_d, w_glu, b_glu, conv_w, gnorm_g, w_o, ln1_g, ln1_b, w_ff1, w_ff2, ln2_g, ln2_b):
    nb, seq, _ = x.shape
    tables = rope_tables(seq)
    mods = adaln(c, w_ada, b_ada)
    w_in_bf16 = w_in.astype(BF16)
    for l in range(DEPTH):
        sh1, sc1, g1, sh2, sc2, g2 = (m[:, None, :] for m in jnp.split(mods[l], 6, axis=-1))
        u, conv_in, q, k, vt, qi, kk, wi = in_proj(x, sc1, sh1, w_in_bf16, l, tables)
        gn = gnorm_g[l]
        ys = s5_group(u.reshape(nb, seq, SSM_DIM), lam_re[l], lam_im[l], log_dt[l], ssm_b_re[l],
                      ssm_b_im[l], ssm_c_re[l], ssm_c_im[l], ssm_d[l], w_glu[l], b_glu[l],
                      gn[:SSM_DIM]).reshape(nb * seq, SSM_DIM)
        yc = conv_group(conv_in, conv_w[l], gn[SSM_DIM:SSM_DIM + CONV_DIM], nb, seq)
        ya = attn_group(q, k, vt, qi, kk, wi, gn[SSM_DIM + CONV_DIM:], nb, seq)
        x = out_proj(x, ys, yc, ya, w_o[l], g1, ln1_g[l], ln1_b[l])
        x = ffn(x, sc2, sh2, w_ff1[l], w_ff2[l], g2, ln2_g[l], ln2_b[l])
    return x
```

```python
import functools
import math

import jax
import jax.numpy as jnp
from jax import lax
from jax.experimental import pallas as pl
from jax.experimental.pallas import tpu as pltpu

D_MODEL = 2048
DEPTH = 2
CHUNK = 64
SSM_DIM = 512
SSM_GROUP_CH = 16
SSM_GROUPS = 32
SSM_STATE = 64
CONV_DIM = 512
CONV_WIDTH = 3
ATTN_DIM = 1024
HEAD_DIM = 128
N_HEADS = 8
IDX_HEADS = 16
IDX_DIM = 64
TOPK_MAX = 256
D_FF = 4 * D_MODEL
ROPE_THETA = 10000.0
ALPHA = (2.0 * DEPTH) ** 0.25
LN_EPS = 1e-5
RMS_EPS = 1e-6

_in_edges = [0, SSM_DIM, SSM_DIM + 3 * CONV_DIM]
_in_edges += [_in_edges[-1] + ATTN_DIM, _in_edges[-1] + 2 * ATTN_DIM, _in_edges[-1] + 3 * ATTN_DIM]
_in_edges += [_in_edges[-1] + IDX_HEADS * IDX_DIM]
_in_edges += [_in_edges[-1] + IDX_DIM + IDX_HEADS]
IN_COLS = {name: (_in_edges[j], _in_edges[j + 1])
           for j, name in enumerate(("u", "conv", "q", "k", "v", "qi", "tail"))}

LANES = 128
SUBLANES = 8
VMEM_LIMIT = 56 * 1024 * 1024

ADA_TN = 1024
PROJ_TM = 256
S5_TC = 128
S5_HALF = SSM_DIM // 2
S5_NH = SSM_GROUPS // 2 * SSM_STATE
CONV_TM = 512
ATT_TQ = 512
ATT_KB = 256
OUT_TM = 512
FFN_TM = 512
FFN_TF = 1024
FFN_TN = 512

BF16 = jnp.bfloat16
F32 = jnp.float32
NEG_BIG = -1e30
Q_SCALE = HEAD_DIM ** -0.5 * math.log2(math.e)


def _cparams(sem):
    return pltpu.CompilerParams(dimension_semantics=sem, vmem_limit_bytes=VMEM_LIMIT)


def _resident(shape):
    nd = len(shape)
    return pl.BlockSpec(shape, lambda *_: (0,) * nd, pipeline_mode=pl.Buffered(1))


def _adaln_kernel(c_ref, w_ref, b_ref, o_ref):
    w = w_ref[0].astype(BF16)
    o_ref[0] = jnp.dot(c_ref[...], w, preferred_element_type=F32) + b_ref[0]


def adaln(c, w_ada, b_ada):
    nb = c.shape[0]
    rows = 16
    cp = jnp.zeros((rows, D_MODEL), BF16).at[:nb].set(c.astype(BF16))
    n_out = w_ada.shape[-1]
    out = pl.pallas_call(
        _adaln_kernel,
        grid=(DEPTH, n_out // ADA_TN),
        in_specs=[
            pl.BlockSpec((rows, D_MODEL), lambda l, j: (0, 0)),
            pl.BlockSpec((1, D_MODEL, ADA_TN), lambda l, j: (l, 0, j)),
            pl.BlockSpec((1, 1, ADA_TN), lambda l, j: (l, 0, j)),
        ],
        out_specs=pl.BlockSpec((1, rows, ADA_TN), lambda l, j: (l, 0, j)),
        out_shape=jax.ShapeDtypeStruct((DEPTH, rows, n_out), F32),
        compiler_params=_cparams(("arbitrary", "arbitrary")),
        name="adaln",
    )(cp, w_ada, b_ada.reshape(DEPTH, 1, n_out))
    return out[:, :nb]


def _rope_halves(x, cos, sin_signed, half):
    if 2 * half == LANES:
        swapped = pltpu.roll(x, half, axis=1)
    else:
        lane = lax.broadcasted_iota(jnp.int32, x.shape, 1)
        first = (lane % (2 * half)) < half
        swapped = jnp.where(first, pltpu.roll(x, LANES - half, axis=1), pltpu.roll(x, half, axis=1))
    return x * cos + swapped * sin_signed


def _in_proj_kernel(x_ref, sc_ref, sh_ref, w_ref, wv_ref, wt_ref,
                    cosa_ref, sina_ref, cosi_ref, sini_ref,
                    u_ref, conv_ref, q_ref, k_ref, vt_ref, qi_ref, kk_ref, wi_ref):
    def proj(name):
        lo, hi = IN_COLS[name]
        return jnp.dot(h, w_ref[:, lo:hi], preferred_element_type=F32)

    h = (x_ref[...] * (1.0 + sc_ref[0]) + sh_ref[0]).astype(BF16)
    u_ref[...] = proj("u")
    conv_ref[...] = proj("conv")
    vt_ref[...] = lax.dot_general(wv_ref[...], h, (((1,), (1,)), ((), ())),
                                  preferred_element_type=F32).astype(BF16)

    cosa, sina = cosa_ref[...], sina_ref[...]
    q = proj("q")
    k = proj("k")
    for hd in range(N_HEADS):
        sl = slice(hd * HEAD_DIM, (hd + 1) * HEAD_DIM)
        q_ref[:, sl] = (_rope_halves(q[:, sl], cosa, sina, HEAD_DIM // 2)
                        * Q_SCALE).astype(BF16)
        k_ref[:, sl] = _rope_halves(k[:, sl], cosa, sina, HEAD_DIM // 2).astype(BF16)

    cosi, sini = cosi_ref[...], sini_ref[...]
    qi = proj("qi")
    for g in range(IDX_HEADS * IDX_DIM // LANES):
        sl = slice(g * LANES, (g + 1) * LANES)
        qi_ref[:, sl] = _rope_halves(qi[:, sl], cosi, sini, IDX_DIM // 2).astype(BF16)

    tail = jnp.dot(h, wt_ref[...], preferred_element_type=F32)
    ki2 = jnp.where(lax.broadcasted_iota(jnp.int32, tail.shape, 1) < IDX_DIM,
                    tail, pltpu.roll(tail, IDX_DIM, axis=1))
    kk_ref[...] = _rope_halves(ki2, cosi, sini, IDX_DIM // 2).astype(BF16)
    wi_ref[...] = pltpu.roll(tail, LANES - IDX_DIM, axis=1) * (
        (IDX_DIM ** -0.5) * (IDX_HEADS ** -0.5))


def in_proj(x, sc, sh, w_in_bf16, layer, tables):
    nb, seq, _ = x.shape
    n = nb * seq
    tm = PROJ_TM
    assert tm == ATT_KB
    tiles_per_seq = seq // tm
    lo, hi = IN_COLS["v"]
    wv = w_in_bf16[layer, :, lo:hi].T
    lo, hi = IN_COLS["tail"]
    wt = jnp.zeros((D_MODEL, LANES), BF16).at[:, :hi - lo].set(w_in_bf16[layer, :, lo:hi])
    main_cols = IN_COLS["tail"][0]
    cosa, sina, cosi, sini = tables

    row = lambda w: pl.BlockSpec((tm, w), lambda i: (i, 0))
    mod = pl.BlockSpec((1, 1, D_MODEL), lambda i: (i // tiles_per_seq, 0, 0))
    tab = pl.BlockSpec((tm, LANES), lambda i: (i % tiles_per_seq, 0))
    outs = pl.pallas_call(
        _in_proj_kernel,
        grid=(n // tm,),
        in_specs=[pl.BlockSpec((None, tm, D_MODEL), lambda i: (i // tiles_per_seq, i % tiles_per_seq, 0)),
                  mod, mod,
                  pl.BlockSpec((None, D_MODEL, main_cols), lambda i: (layer, 0, 0),
                               pipeline_mode=pl.Buffered(1)),
                  _resident(wv.shape), _resident(wt.shape),
                  tab, tab, tab, tab],
        out_specs=[row(SSM_DIM), row(3 * CONV_DIM), row(ATTN_DIM), row(ATTN_DIM),
                   pl.BlockSpec((None, None, ATTN_DIM, tm),
                                lambda i: (i // tiles_per_seq, i % tiles_per_seq, 0, 0)),
                   row(IDX_HEADS * IDX_DIM), row(LANES), row(LANES)],
        out_shape=[jax.ShapeDtypeStruct((n, SSM_DIM), F32),
                   jax.ShapeDtypeStruct((n, 3 * CONV_DIM), F32),
                   jax.ShapeDtypeStruct((n, ATTN_DIM), BF16),
                   jax.ShapeDtypeStruct((n, ATTN_DIM), BF16),
                   jax.ShapeDtypeStruct((nb, tiles_per_seq, ATTN_DIM, tm), BF16),
                   jax.ShapeDtypeStruct((n, IDX_HEADS * IDX_DIM), BF16),
                   jax.ShapeDtypeStruct((n, LANES), BF16),
                   jax.ShapeDtypeStruct((n, LANES), F32)],
        compiler_params=_cparams(("arbitrary",)),
        name="in_proj",
    )(x, sc, sh, w_in_bf16, wv, wt, cosa, sina, cosi, sini)
    return outs


def rope_tables(seq):
    def tab(dim):
        inv = 1.0 / (ROPE_THETA ** (jnp.arange(0, dim, 2, dtype=F32) / dim))
        ang = jnp.arange(seq, dtype=F32)[:, None] * inv[None, :]
        cos, sin = jnp.cos(ang), jnp.sin(ang)
        reps = LANES // dim
        return (jnp.tile(jnp.concatenate([cos, cos], axis=1), (1, reps)),
                jnp.tile(jnp.concatenate([-sin, sin], axis=1), (1, reps)))
    cosa, sina = tab(HEAD_DIM)
    cosi, sini = tab(IDX_DIM)
    return cosa, sina, cosi, sini


def _s5_kernel(u_ref, bb_ref, cc_ref, are_ref, aim_ref, d_ref, wg_ref, bg_ref, gn_ref,
               o_ref, st_ref, xs_ref, y_ref, lhs_ref, *, nb):
    rows_t = 2 * nb
    tc = xs_ref.shape[0] // rows_t
    n_slab = SSM_DIM // LANES

    def half_rows(b, s):
        return pl.ds(2 * b + s // (n_slab // 2), tc, stride=rows_t)

    @pl.when(pl.program_id(0) == 0)
    def _():
        st_ref[...] = jnp.zeros_like(st_ref)
        lhs_ref[...] = jnp.zeros_like(lhs_ref)

    for b in range(nb):
        for s in range(n_slab):
            lhs_ref[s, half_rows(b, s), :] = u_ref[b, :, s * LANES:(s + 1) * LANES]
    lhs = jnp.concatenate([lhs_ref[s] for s in range(n_slab)], axis=1).astype(BF16)
    xs_ref[...] = jnp.dot(lhs, bb_ref[...], preferred_element_type=F32)
    a_re, a_im = are_ref[...], aim_ref[...]

    def step(t, carry):
        xr, xi = carry
        r0 = pl.multiple_of(t * rows_t, rows_t)
        bur = xs_ref[pl.ds(r0, rows_t), :S5_NH]
        bui = xs_ref[pl.ds(r0, rows_t), S5_NH:]
        nr = a_re * xr - a_im * xi + bur
        ni = a_re * xi + a_im * xr + bui
        xs_ref[pl.ds(r0, rows_t), :S5_NH] = nr
        xs_ref[pl.ds(r0, rows_t), S5_NH:] = ni
        return nr, ni

    xr, xi = lax.fori_loop(0, tc, step, (st_ref[0], st_ref[1]), unroll=4)
    st_ref[0] = xr
    st_ref[1] = xi

    yy = jnp.dot(xs_ref[...].astype(BF16), cc_ref[...], preferred_element_type=F32)
    for s in range(n_slab):
        y_ref[s] = yy[:, s * LANES:(s + 1) * LANES]
    y = jnp.concatenate(
        [jnp.concatenate([y_ref[s, half_rows(b, s), :] for s in range(n_slab)], axis=1)
         for b in range(nb)], axis=0)
    y = y + d_ref[...] * u_ref[...].reshape(nb * tc, SSM_DIM)
    g = jax.nn.gelu(y)
    z = jnp.dot(g.astype(BF16), wg_ref[...], preferred_element_type=F32) + bg_ref[...]
    out = g * jax.nn.sigmoid(z)
    out = out * lax.rsqrt(jnp.mean(out * out, axis=-1, keepdims=True) + RMS_EPS) * gn_ref[...]
    o_ref[...] = out.reshape(nb, tc, SSM_DIM).astype(BF16)


def s5_group(u, lam_re, lam_im, log_dt, b_re, b_im, c_re, c_im, d_skip, w_glu, b_glu, gn):
    nb, seq, _ = u.shape
    assert 2 * nb == SUBLANES
    tc = S5_TC
    dt = jnp.exp(log_dt)[:, None]
    mag = jnp.exp(lam_re * dt)
    ang = lam_im * dt
    lb_re, lb_im = mag * jnp.cos(ang), mag * jnp.sin(ang)
    den = lam_re * lam_re + lam_im * lam_im
    n_re, n_im = lb_re - 1.0, lb_im
    f_re = (n_re * lam_re + n_im * lam_im) / den
    f_im = (n_im * lam_re - n_re * lam_im) / den
    bb_re = f_re[..., None] * b_re - f_im[..., None] * b_im
    bb_im = f_re[..., None] * b_im + f_im[..., None] * b_re
    gh = SSM_GROUPS // 2
    eye = jnp.eye(gh, dtype=F32)

    def in_mat(m):
        m = m.reshape(2, gh, SSM_STATE, SSM_GROUP_CH)
        bd = jnp.einsum('rgph,gk->rghkp', m, eye)
        return bd.reshape(SSM_DIM, S5_NH)

    def out_mat(m):
        m = m.reshape(2, gh, SSM_GROUP_CH, SSM_STATE)
        bd = jnp.einsum('rghp,gk->kprgh', m, eye)
        return bd.reshape(S5_NH, SSM_DIM)

    bb = jnp.concatenate([in_mat(bb_re), in_mat(bb_im)], axis=1).astype(BF16)
    cc = jnp.concatenate([out_mat(c_re), -out_mat(c_im)], axis=0).astype(BF16)

    def lane_vec(m):
        return jnp.tile(m.reshape(2, S5_NH), (nb, 1))

    a_re, a_im = lane_vec(lb_re), lane_vec(lb_im)


    rows = tc * 2 * nb
    out = pl.pallas_call(
        functools.partial(_s5_kernel, nb=nb),
        grid=(seq // tc,),
        in_specs=[
            pl.BlockSpec((nb, tc, SSM_DIM), lambda i: (0, i, 0)),
            _resident(bb.shape), _resident(cc.shape),
            _resident(a_re.shape), _resident(a_im.shape),
            _resident((1, SSM_DIM)), _resident((SSM_DIM, SSM_DIM)),
            _resident((1, SSM_DIM)), _resident((1, SSM_DIM)),
        ],
        out_specs=pl.BlockSpec((nb, tc, SSM_DIM), lambda i: (0, i, 0)),
        out_shape=jax.ShapeDtypeStruct((nb, seq, SSM_DIM), BF16),
        scratch_shapes=[pltpu.VMEM((2, 2 * nb, S5_NH), F32),
                        pltpu.VMEM((rows, 2 * S5_NH), F32),
                        pltpu.VMEM((SSM_DIM // LANES, rows, LANES), F32),
                        pltpu.VMEM((SSM_DIM // LANES, rows, LANES), F32)],
        compiler_params=_cparams(("arbitrary",)),
        name="s5",
    )(u, bb, cc, a_re, a_im, d_skip.reshape(1, -1), w_glu.astype(BF16),
      b_glu.reshape(1, -1), gn.reshape(1, -1))
    return out


def _conv_kernel(c_ref, w_ref, gn_ref, o_ref, zp_ref):
    tm = o_ref.shape[0]
    ch = c_ref[:, :CONV_DIM]
    gb = c_ref[:, CONV_DIM:2 * CONV_DIM]
    gc = c_ref[:, 2 * CONV_DIM:]

    @pl.when(pl.program_id(1) == 0)
    def _():
        zp_ref[pl.ds(0, SUBLANES), :] = jnp.zeros((SUBLANES, CONV_DIM), F32)

    @pl.when(pl.program_id(1) != 0)
    def _():
        zp_ref[pl.ds(0, SUBLANES), :] = zp_ref[pl.ds(tm, SUBLANES), :]

    zp_ref[pl.ds(SUBLANES, tm), :] = gc * ch
    acc = zp_ref[pl.ds(SUBLANES, tm), :] * w_ref[2:3, :]
    acc += zp_ref[pl.ds(SUBLANES - 1, tm), :] * w_ref[1:2, :]
    acc += zp_ref[pl.ds(SUBLANES - 2, tm), :] * w_ref[0:1, :]
    y = gb * acc
    y = y * lax.rsqrt(jnp.mean(y * y, axis=-1, keepdims=True) + RMS_EPS) * gn_ref[...]
    o_ref[...] = y.astype(BF16)


def conv_group(conv_in, conv_w, gn, nb, seq):
    tm = CONV_TM
    tiles = seq // tm
    return pl.pallas_call(
        _conv_kernel,
        grid=(nb, tiles),
        in_specs=[pl.BlockSpec((tm, 3 * CONV_DIM), lambda b, i: (b * tiles + i, 0)),
                  pl.BlockSpec((CONV_WIDTH, CONV_DIM), lambda b, i: (0, 0)),
                  pl.BlockSpec((1, CONV_DIM), lambda b, i: (0, 0))],
        out_specs=pl.BlockSpec((tm, CONV_DIM), lambda b, i: (b * tiles + i, 0)),
        out_shape=jax.ShapeDtypeStruct((nb * seq, CONV_DIM), BF16),
        scratch_shapes=[pltpu.VMEM((tm + SUBLANES, CONV_DIM), F32)],
        compiler_params=_cparams(("arbitrary", "arbitrary")),
        name="conv",
    )(conv_in, conv_w, gn.reshape(1, -1))


def _key_to_f32(key):
    bits = key ^ ((key >> 31) & jnp.int32(0x7FFFFFFF))
    return pltpu.bitcast(bits, F32)


def _attn_kernel(q_ref, qi_ref, wi_ref, k_ref, vt_ref, kk_ref, gn_ref, o_ref,
                 sc_ref, lhs_ref, wib_ref, m_ref, l_ref, a_ref, acc_ref, s_ref, p_ref, *, topk):
    tq, kb_sz = ATT_TQ, ATT_KB
    i = pl.program_id(1)
    n_kb = (i + 1) * (tq // kb_sz)
    lane128 = lax.broadcasted_iota(jnp.int32, (tq, LANES), 1)

    for h in range(IDX_HEADS):
        pair = qi_ref[:, (h // 2) * LANES:(h // 2 + 1) * LANES]
        mine = (lane128 // IDX_DIM) == (h % 2)
        lhs_ref[h] = jnp.where(mine, pair, jnp.zeros_like(pair))
        wib_ref[h] = jnp.broadcast_to(wi_ref[:, h:h + 1], (tq, LANES))

    key_row = lax.broadcasted_iota(jnp.int32, (kb_sz, tq), 0)
    qry_lane = lax.broadcasted_iota(jnp.int32, (kb_sz, tq), 1)
    key_chunk = key_row // CHUNK
    qry_chunk = qry_lane // CHUNK + i * (tq // CHUNK)

    def admissible(kb):
        return key_chunk + kb * (kb_sz // CHUNK) <= qry_chunk

    def score_block(kb, _):
        k0 = pl.multiple_of(kb * kb_sz, kb_sz)
        kk = kk_ref[pl.ds(k0, kb_sz), :]
        acc = jnp.zeros((tq, kb_sz), F32)
        for h in range(IDX_HEADS):
            logit = lax.dot_general(lhs_ref[h], kk, (((1,), (1,)), ((), ())),
                                    preferred_element_type=F32)
            w = wib_ref[h]
            acc = acc + jnp.maximum(logit, 0.0) * jnp.concatenate([w] * (kb_sz // LANES), axis=1)
        sc_ref[kb] = jnp.where(admissible(kb), acc.T, -jnp.inf)
        return 0

    lax.fori_loop(0, n_kb, score_block, 0)

    @pl.when(n_kb % 2 == 1)
    def _():
        sc_ref[n_kb] = jnp.full((kb_sz, tq), -jnp.inf, F32)

    def count_if(pred):
        def one(kb, cnt):
            hit = jnp.where(pred(sc_ref[kb], kb * kb_sz), 1.0, 0.0)
            return cnt + jnp.sum(hit.reshape(kb_sz // SUBLANES, SUBLANES, tq), axis=0)

        def pair(j, cnt):
            return one(2 * j + 1, one(2 * j, cnt))

        cnt = lax.fori_loop(0, (n_kb + 1) // 2, pair, jnp.zeros((SUBLANES, tq), F32))
        return jnp.sum(cnt, axis=0, keepdims=True)

    k_f = jnp.float32(topk)
    q_lane = lax.broadcasted_iota(jnp.int32, (1, tq), 1)
    n_adm = ((i * (tq // CHUNK) + q_lane // CHUNK + 1) * CHUNK).astype(F32)
    searched = n_adm > k_f
    c0 = count_if(lambda s, _: s >= 0.0)
    pos = c0 >= k_f
    thr0 = jnp.where(pos, jnp.int32(0), jnp.int32(-2 ** 31))
    cnt0 = jnp.where(pos, c0, (n_kb * kb_sz).astype(F32))

    def unresolved(cnt_thr):
        return jnp.max(jnp.where(searched, cnt_thr, k_f)) > k_f

    def bisect_cond(st):
        b, _, cnt_thr = st
        return jnp.logical_and(b < 31, unresolved(cnt_thr))

    def bisect(st):
        b, thr, cnt_thr = st
        cand = thr + (jnp.int32(1) << (30 - b))
        cand_f = _key_to_f32(cand)
        c = count_if(lambda s, _: s >= cand_f)
        ok = c >= k_f
        return b + 1, jnp.where(ok, cand, thr), jnp.where(ok, c, cnt_thr)

    _, thr, cnt_thr = lax.while_loop(bisect_cond, bisect, (jnp.int32(0), thr0, cnt0))
    thr_f = jnp.where(searched, _key_to_f32(thr), -jnp.inf)

    def tie_cut(_):
        need = k_f - count_if(lambda s, _: s > thr_f)
        n_bits = (sc_ref.shape[0] * kb_sz - 1).bit_length()

        def body(b, m):
            step = jnp.int32(1) << (n_bits - 1 - b)
            top = m + step - 1
            c = count_if(lambda s, k0: jnp.logical_and(s == thr_f, key_row + k0 <= top))
            return jnp.where(c < need, m + step, m)

        return lax.fori_loop(0, n_bits, body, jnp.zeros((1, tq), jnp.int32))

    idx_cut = lax.cond(unresolved(cnt_thr), tie_cut,
                       lambda _: jnp.full((1, tq), 2 ** 30, jnp.int32), 0)

    def bias_block(kb, _):
        s = sc_ref[kb]
        sel = jnp.logical_or(s > thr_f, jnp.logical_and(s == thr_f, key_row + kb * kb_sz <= idx_cut))
        sel = jnp.logical_and(sel, admissible(kb))
        sc_ref[kb] = jnp.where(sel, 0.0, -jnp.inf)
        return 0

    lax.fori_loop(0, n_kb, bias_block, 0)

    m_ref[...] = jnp.full(m_ref.shape, NEG_BIG, F32)
    l_ref[...] = jnp.zeros(l_ref.shape, F32)
    acc_ref[...] = jnp.zeros(acc_ref.shape, F32)

    def att_block(kb, _):
        k0 = pl.multiple_of(kb * kb_sz, kb_sz)
        bias = sc_ref[kb]
        for h in range(N_HEADS):
            hs = slice(h * HEAD_DIM, (h + 1) * HEAD_DIM)
            s_ref[h] = lax.dot_general(k_ref[pl.ds(k0, kb_sz), hs], q_ref[:, hs],
                                       (((1,), (1,)), ((), ())),
                                       preferred_element_type=F32) + bias
        for h in range(N_HEADS):
            s = s_ref[h]
            m_old = m_ref[h]
            m_new = jnp.maximum(m_old, jnp.max(s, axis=0, keepdims=True))
            p = jnp.exp2(s - m_new)
            p_ref[h] = p.astype(BF16)
            a_ref[h] = jnp.exp2(m_old - m_new)
            m_ref[h] = m_new
        ones = jnp.ones((2 * SUBLANES, kb_sz), BF16)
        for h in range(N_HEADS):
            hs = slice(h * HEAD_DIM, (h + 1) * HEAD_DIM)
            pv = jnp.dot(jnp.concatenate([vt_ref[kb, hs, :], ones], axis=0), p_ref[h],
                         preferred_element_type=F32)
            acc_ref[h] = a_ref[h] * acc_ref[h] + pv[:HEAD_DIM]
            l_ref[h] = a_ref[h] * l_ref[h] + pv[HEAD_DIM:HEAD_DIM + 1]
        return 0

    lax.fori_loop(0, n_kb, att_block, 0)

    ssq = jnp.zeros((1, tq), F32)
    for h in range(N_HEADS):
        yh = acc_ref[h] / l_ref[h]
        acc_ref[h] = yh
        ssq = ssq + jnp.sum(yh * yh, axis=0, keepdims=True)
    scale = lax.rsqrt(ssq / ATTN_DIM + RMS_EPS)
    for h in range(N_HEADS):
        hs = slice(h * HEAD_DIM, (h + 1) * HEAD_DIM)
        o_ref[:, hs] = ((acc_ref[h] * scale).T * gn_ref[:, hs]).astype(BF16)


def attn_group(q, k, vt, qi, kk, wi, gn, nb, seq):
    tq = ATT_TQ
    tiles = seq // tq
    topk = min(TOPK_MAX, seq // 4)
    assert topk <= ATT_KB and ATT_TQ % ATT_KB == 0 and ATT_KB % CHUNK == 0 and (seq // ATT_KB) % 2 == 0
    qrow = lambda w: pl.BlockSpec((tq, w), lambda b, i: (b * tiles + i, 0))
    whole = lambda w: pl.BlockSpec((seq, w), lambda b, i: (b, 0), pipeline_mode=pl.Buffered(1))
    vt_spec = pl.BlockSpec((None, seq // ATT_KB, ATTN_DIM, ATT_KB), lambda b, i: (b, 0, 0, 0),
                           pipeline_mode=pl.Buffered(1))
    return pl.pallas_call(
        functools.partial(_attn_kernel, topk=topk),
        grid=(nb, tiles),
        in_specs=[qrow(ATTN_DIM), qrow(IDX_HEADS * IDX_DIM), qrow(LANES),
                  whole(ATTN_DIM), vt_spec, whole(LANES),
                  pl.BlockSpec((1, ATTN_DIM), lambda b, i: (0, 0))],
        out_specs=qrow(ATTN_DIM),
        out_shape=jax.ShapeDtypeStruct((nb * seq, ATTN_DIM), BF16),
        scratch_shapes=[pltpu.VMEM((seq // ATT_KB, ATT_KB, tq), F32),
                        pltpu.VMEM((IDX_HEADS, tq, LANES), BF16),
                        pltpu.VMEM((IDX_HEADS, tq, LANES), F32),
                        pltpu.VMEM((N_HEADS, 1, tq), F32),
                        pltpu.VMEM((N_HEADS, 1, tq), F32),
                        pltpu.VMEM((N_HEADS, 1, tq), F32),
                        pltpu.VMEM((N_HEADS, HEAD_DIM, tq), F32),
                        pltpu.VMEM((N_HEADS, ATT_KB, tq), F32),
                        pltpu.VMEM((N_HEADS, ATT_KB, tq), BF16)],
        compiler_params=_cparams(("arbitrary", "arbitrary")),
        name="attn",
    )(q, qi, wi, k, vt, kk, gn.reshape(1, -1))


def _layer_norm(r, g, b):
    mu = jnp.mean(r, axis=-1, keepdims=True)
    d = r - mu
    var = jnp.mean(d * d, axis=-1, keepdims=True)
    return d * lax.rsqrt(var + LN_EPS) * g + b


def _out_proj_kernel(x_ref, ys_ref, yc_ref, ya_ref, w_ref, gate_ref, g_ref, b_ref, o_ref):
    mix = jnp.dot(ys_ref[...], w_ref[pl.ds(0, SSM_DIM), :], preferred_element_type=F32)
    mix += jnp.dot(yc_ref[...], w_ref[pl.ds(SSM_DIM, CONV_DIM), :], preferred_element_type=F32)
    mix += jnp.dot(ya_ref[...], w_ref[pl.ds(SSM_DIM + CONV_DIM, ATTN_DIM), :],
                   preferred_element_type=F32)
    r = ALPHA * x_ref[...] + gate_ref[0] * mix
    o_ref[...] = _layer_norm(r, g_ref[...], b_ref[...])


def out_proj(x, ys, yc, ya, w_o, gate, ln_g, ln_b):
    nb, seq, _ = x.shape
    tm = OUT_TM
    tiles = seq // tm
    row = lambda w: pl.BlockSpec((tm, w), lambda i: (i, 0))
    xrow = pl.BlockSpec((None, tm, D_MODEL), lambda i: (i // tiles, i % tiles, 0))
    vec = pl.BlockSpec((1, D_MODEL), lambda i: (0, 0))
    return pl.pallas_call(
        _out_proj_kernel,
        grid=(nb * tiles,),
        in_specs=[xrow, row(SSM_DIM), row(CONV_DIM), row(ATTN_DIM),
                  _resident((D_MODEL, D_MODEL)),
                  pl.BlockSpec((1, 1, D_MODEL), lambda i: (i // tiles, 0, 0)), vec, vec],
        out_specs=xrow,
        out_shape=jax.ShapeDtypeStruct((nb, seq, D_MODEL), F32),
        compiler_params=_cparams(("arbitrary",)),
        name="out_proj",
    )(x, ys, yc, ya, w_o.astype(BF16), gate, ln_g.reshape(1, -1), ln_b.reshape(1, -1))


def _ffn_kernel(x_ref, sc_ref, sh_ref, w1_ref, w2_ref, gate_ref, g_ref, b_ref, o_ref,
                h_ref, acc_ref):
    j = pl.program_id(1)

    @pl.when(j == 0)
    def _():
        h_ref[...] = (x_ref[...] * (1.0 + sc_ref[0]) + sh_ref[0]).astype(BF16)
        acc_ref[...] = jnp.zeros_like(acc_ref)

    a = jnp.maximum(jnp.dot(h_ref[...], w1_ref[...], preferred_element_type=F32), 0.0)
    a = (a * a).astype(BF16)
    for c in range(D_MODEL // FFN_TN):
        cs = slice(c * FFN_TN, (c + 1) * FFN_TN)
        acc_ref[:, cs] += jnp.dot(a, w2_ref[:, cs], preferred_element_type=F32)

    @pl.when(j == pl.num_programs(1) - 1)
    def _():
        r = ALPHA * x_ref[...] + gate_ref[0] * acc_ref[...]
        o_ref[...] = _layer_norm(r, g_ref[...], b_ref[...])


def ffn(x, sc, sh, w1_bf16, w2_bf16, layer, gate, ln_g, ln_b):
    nb, seq, _ = x.shape
    tm, tf = FFN_TM, FFN_TF
    tiles = seq // tm
    row = pl.BlockSpec((None, tm, D_MODEL), lambda i, j: (i // tiles, i % tiles, 0))
    mod = pl.BlockSpec((1, 1, D_MODEL), lambda i, j: (i // tiles, 0, 0))
    vec = pl.BlockSpec((1, D_MODEL), lambda i, j: (0, 0))
    return pl.pallas_call(
        _ffn_kernel,
        grid=(nb * tiles, D_FF // tf),
        in_specs=[row, mod, mod,
                  pl.BlockSpec((None, D_MODEL, tf), lambda i, j: (layer, 0, j)),
                  pl.BlockSpec((None, tf, D_MODEL), lambda i, j: (layer, j, 0)),
                  mod, vec, vec],
        out_specs=row,
        out_shape=jax.ShapeDtypeStruct((nb, seq, D_MODEL), F32),
        scratch_shapes=[pltpu.VMEM((tm, D_MODEL), BF16), pltpu.VMEM((tm, D_MODEL), F32)],
        compiler_params=_cparams(("arbitrary", "arbitrary")),
        name="ffn",
    )(x, sc, sh, w1_bf16, w2_bf16, gate, ln_g.reshape(1, -1), ln_b.reshape(1, -1))


def kernel(x, c, w_ada, b_ada, w_in, lam_re, lam_im, log_dt, ssm_b_re, ssm_b_im, ssm_c_re, ssm_c_im,
           ssm_d, w_glu, b_glu, conv_w, gnorm_g, w_o, ln1_g, ln1_b, w_ff1, w_ff2, ln2_g, ln2_b):
    nb, seq, _ = x.shape
    tables = rope_tables(seq)
    mods = adaln(c, w_ada, b_ada)
    w_in_bf16 = w_in.astype(BF16)
    w1_bf16, w2_bf16 = w_ff1.astype(BF16), w_ff2.astype(BF16)
    for l in range(DEPTH):
        sh1, sc1, g1, sh2, sc2, g2 = (m[:, None, :] for m in jnp.split(mods[l], 6, axis=-1))
        u, conv_in, q, k, vt, qi, kk, wi = in_proj(x, sc1, sh1, w_in_bf16, l, tables)
        gn = gnorm_g[l]
        ys = s5_group(u.reshape(nb, seq, SSM_DIM), lam_re[l], lam_im[l], log_dt[l], ssm_b_re[l],
                      ssm_b_im[l], ssm_c_re[l], ssm_c_im[l], ssm_d[l], w_glu[l], b_glu[l],
                      gn[:SSM_DIM]).reshape(nb * seq, SSM_DIM)
        yc = conv_group(conv_in, conv_w[l], gn[SSM_DIM:SSM_DIM + CONV_DIM], nb, seq)
        ya = attn_group(q, k, vt, qi, kk, wi, gn[SSM_DIM + CONV_DIM:], nb, seq)
        x = out_proj(x, ys, yc, ya, w_o[l], g1, ln1_g[l], ln1_b[l])
        x = ffn(x, sc2, sh2, w1_bf16, w2_bf16, l, g2, ln2_g[l], ln2_b[l])
    return x
```

```python
import functools
import math

import jax
import jax.numpy as jnp
from jax import lax
from jax.experimental import pallas as pl
from jax.experimental.pallas import tpu as pltpu

D_MODEL = 2048
DEPTH = 2
CHUNK = 64
SSM_DIM = 512
SSM_GROUP_CH = 16
SSM_GROUPS = 32
SSM_STATE = 64
CONV_DIM = 512
CONV_WIDTH = 3
ATTN_DIM = 1024
HEAD_DIM = 128
N_HEADS = 8
IDX_HEADS = 16
IDX_DIM = 64
TOPK_MAX = 256
D_FF = 4 * D_MODEL
ROPE_THETA = 10000.0
ALPHA = (2.0 * DEPTH) ** 0.25
LN_EPS = 1e-5
RMS_EPS = 1e-6

_in_edges = [0, SSM_DIM, SSM_DIM + 3 * CONV_DIM]
_in_edges += [_in_edges[-1] + ATTN_DIM, _in_edges[-1] + 2 * ATTN_DIM, _in_edges[-1] + 3 * ATTN_DIM]
_in_edges += [_in_edges[-1] + IDX_HEADS * IDX_DIM]
_in_edges += [_in_edges[-1] + IDX_DIM + IDX_HEADS]
IN_COLS = {name: (_in_edges[j], _in_edges[j + 1])
           for j, name in enumerate(("u", "conv", "q", "k", "v", "qi", "tail"))}

LANES = 128
SUBLANES = 8
VMEM_CAPACITY = 64 * 1024 * 1024
VMEM_LIMIT = 56 * 1024 * 1024
VMEM_LIMIT_ATTN = VMEM_CAPACITY - 4 * 1024 * 1024

ADA_TN = 1024
PROJ_TM = 256
S5_TC = 128
S5_HALF = SSM_DIM // 2
S5_NH = SSM_GROUPS // 2 * SSM_STATE
CONV_TM = 512
ATT_TQ = 512
ATT_KB = 256
OUT_TM = 512
FFN_TM = 512
FFN_TF = 1024
FFN_TN = 512

BF16 = jnp.bfloat16
F32 = jnp.float32
NEG_BIG = -1e30
Q_SCALE = HEAD_DIM ** -0.5 * math.log2(math.e)


def _cparams(sem, vmem_limit=VMEM_LIMIT):
    return pltpu.CompilerParams(dimension_semantics=sem, vmem_limit_bytes=vmem_limit)


def _resident(shape):
    nd = len(shape)
    return pl.BlockSpec(shape, lambda *_: (0,) * nd, pipeline_mode=pl.Buffered(1))


def _adaln_kernel(c_ref, w_ref, b_ref, o_ref):
    w = w_ref[0].astype(BF16)
    o_ref[0] = jnp.dot(c_ref[...], w, preferred_element_type=F32) + b_ref[0]


def adaln(c, w_ada, b_ada):
    nb = c.shape[0]
    rows = 16
    cp = jnp.zeros((rows, D_MODEL), BF16).at[:nb].set(c.astype(BF16))
    n_out = w_ada.shape[-1]
    out = pl.pallas_call(
        _adaln_kernel,
        grid=(DEPTH, n_out // ADA_TN),
        in_specs=[
            pl.BlockSpec((rows, D_MODEL), lambda l, j: (0, 0)),
            pl.BlockSpec((1, D_MODEL, ADA_TN), lambda l, j: (l, 0, j)),
            pl.BlockSpec((1, 1, ADA_TN), lambda l, j: (l, 0, j)),
        ],
        out_specs=pl.BlockSpec((1, rows, ADA_TN), lambda l, j: (l, 0, j)),
        out_shape=jax.ShapeDtypeStruct((DEPTH, rows, n_out), F32),
        compiler_params=_cparams(("arbitrary", "arbitrary")),
        name="adaln",
    )(cp, w_ada, b_ada.reshape(DEPTH, 1, n_out))
    return out[:, :nb]


def _rope_halves(x, cos, sin_signed, half):
    if 2 * half == LANES:
        swapped = pltpu.roll(x, half, axis=1)
    else:
        lane = lax.broadcasted_iota(jnp.int32, x.shape, 1)
        first = (lane % (2 * half)) < half
        swapped = jnp.where(first, pltpu.roll(x, LANES - half, axis=1), pltpu.roll(x, half, axis=1))
    return x * cos + swapped * sin_signed


def _in_proj_kernel(x_ref, sc_ref, sh_ref, w_ref, wv_ref, wt_ref,
                    cosa_ref, sina_ref, cosi_ref, sini_ref,
                    u_ref, conv_ref, q_ref, k_ref, vt_ref, qi_ref, kk_ref, wi_ref):
    def proj(name):
        lo, hi = IN_COLS[name]
        return jnp.dot(h, w_ref[:, lo:hi], preferred_element_type=F32)

    h = (x_ref[...] * (1.0 + sc_ref[0]) + sh_ref[0]).astype(BF16)
    u_ref[...] = proj("u")
    conv_ref[...] = proj("conv")
    vt_ref[...] = lax.dot_general(wv_ref[...], h, (((1,), (1,)), ((), ())),
                                  preferred_element_type=F32).astype(BF16)

    cosa, sina = cosa_ref[...], sina_ref[...]
    q = proj("q")
    k = proj("k")
    for hd in range(N_HEADS):
        sl = slice(hd * HEAD_DIM, (hd + 1) * HEAD_DIM)
        q_ref[:, sl] = (_rope_halves(q[:, sl], cosa, sina, HEAD_DIM // 2)
                        * Q_SCALE).astype(BF16)
        k_ref[:, sl] = _rope_halves(k[:, sl], cosa, sina, HEAD_DIM // 2).astype(BF16)

    cosi, sini = cosi_ref[...], sini_ref[...]
    qi = proj("qi")
    for g in range(IDX_HEADS * IDX_DIM // LANES):
        sl = slice(g * LANES, (g + 1) * LANES)
        qi_ref[:, sl] = _rope_halves(qi[:, sl], cosi, sini, IDX_DIM // 2).astype(BF16)

    tail = jnp.dot(h, wt_ref[...], preferred_element_type=F32)
    ki2 = jnp.where(lax.broadcasted_iota(jnp.int32, tail.shape, 1) < IDX_DIM,
                    tail, pltpu.roll(tail, IDX_DIM, axis=1))
    kk_ref[...] = _rope_halves(ki2, cosi, sini, IDX_DIM // 2).astype(BF16)
    wi_ref[...] = pltpu.roll(tail, LANES - IDX_DIM, axis=1) * (
        (IDX_DIM ** -0.5) * (IDX_HEADS ** -0.5))


def in_proj(x, sc, sh, w_in_bf16, layer, tables):
    nb, seq, _ = x.shape
    n = nb * seq
    tm = PROJ_TM
    assert tm == ATT_KB
    tiles_per_seq = seq // tm
    lo, hi = IN_COLS["v"]
    wv = w_in_bf16[layer, :, lo:hi].T
    lo, hi = IN_COLS["tail"]
    wt = jnp.zeros((D_MODEL, LANES), BF16).at[:, :hi - lo].set(w_in_bf16[layer, :, lo:hi])
    main_cols = IN_COLS["tail"][0]
    cosa, sina, cosi, sini = tables

    row = lambda w: pl.BlockSpec((tm, w), lambda i: (i, 0))
    mod = pl.BlockSpec((1, 1, D_MODEL), lambda i: (i // tiles_per_seq, 0, 0))
    tab = pl.BlockSpec((tm, LANES), lambda i: (i % tiles_per_seq, 0))
    outs = pl.pallas_call(
        _in_proj_kernel,
        grid=(n // tm,),
        in_specs=[pl.BlockSpec((None, tm, D_MODEL), lambda i: (i // tiles_per_seq, i % tiles_per_seq, 0)),
                  mod, mod,
                  pl.BlockSpec((None, D_MODEL, main_cols), lambda i: (layer, 0, 0),
                               pipeline_mode=pl.Buffered(1)),
                  _resident(wv.shape), _resident(wt.shape),
                  tab, tab, tab, tab],
        out_specs=[row(SSM_DIM), row(3 * CONV_DIM), row(ATTN_DIM), row(ATTN_DIM),
                   pl.BlockSpec((None, None, ATTN_DIM, tm),
                                lambda i: (i // tiles_per_seq, i % tiles_per_seq, 0, 0)),
                   row(IDX_HEADS * IDX_DIM), row(LANES), row(LANES)],
        out_shape=[jax.ShapeDtypeStruct((n, SSM_DIM), F32),
                   jax.ShapeDtypeStruct((n, 3 * CONV_DIM), F32),
                   jax.ShapeDtypeStruct((n, ATTN_DIM), BF16),
                   jax.ShapeDtypeStruct((n, ATTN_DIM), BF16),
                   jax.ShapeDtypeStruct((nb, tiles_per_seq, ATTN_DIM, tm), BF16),
                   jax.ShapeDtypeStruct((n, IDX_HEADS * IDX_DIM), BF16),
                   jax.ShapeDtypeStruct((n, LANES), BF16),
                   jax.ShapeDtypeStruct((n, LANES), F32)],
        compiler_params=_cparams(("arbitrary",)),
        name="in_proj",
    )(x, sc, sh, w_in_bf16, wv, wt, cosa, sina, cosi, sini)
    return outs


def rope_tables(seq):
    def tab(dim):
        inv = 1.0 / (ROPE_THETA ** (jnp.arange(0, dim, 2, dtype=F32) / dim))
        ang = jnp.arange(seq, dtype=F32)[:, None] * inv[None, :]
        cos, sin = jnp.cos(ang), jnp.sin(ang)
        reps = LANES // dim
        return (jnp.tile(jnp.concatenate([cos, cos], axis=1), (1, reps)),
                jnp.tile(jnp.concatenate([-sin, sin], axis=1), (1, reps)))
    cosa, sina = tab(HEAD_DIM)
    cosi, sini = tab(IDX_DIM)
    return cosa, sina, cosi, sini


def _s5_kernel(u_ref, bb_ref, cc_ref, are_ref, aim_ref, d_ref, wg_ref, bg_ref, gn_ref,
               o_ref, st_ref, xs_ref, y_ref, lhs_ref, *, nb):
    rows_t = 2 * nb
    tc = xs_ref.shape[0] // rows_t
    n_slab = SSM_DIM // LANES

    def half_rows(b, s):
        return pl.ds(2 * b + s // (n_slab // 2), tc, stride=rows_t)

    @pl.when(pl.program_id(0) == 0)
    def _():
        st_ref[...] = jnp.zeros_like(st_ref)
        lhs_ref[...] = jnp.zeros_like(lhs_ref)

    for b in range(nb):
        for s in range(n_slab):
            lhs_ref[s, half_rows(b, s), :] = u_ref[b, :, s * LANES:(s + 1) * LANES]
    lhs = jnp.concatenate([lhs_ref[s] for s in range(n_slab)], axis=1).astype(BF16)
    xs_ref[...] = jnp.dot(lhs, bb_ref[...], preferred_element_type=F32)
    a_re, a_im = are_ref[...], aim_ref[...]

    def step(t, carry):
        xr, xi = carry
        r0 = pl.multiple_of(t * rows_t, rows_t)
        bur = xs_ref[pl.ds(r0, rows_t), :S5_NH]
        bui = xs_ref[pl.ds(r0, rows_t), S5_NH:]
        nr = a_re * xr - a_im * xi + bur
        ni = a_re * xi + a_im * xr + bui
        xs_ref[pl.ds(r0, rows_t), :S5_NH] = nr
        xs_ref[pl.ds(r0, rows_t), S5_NH:] = ni
        return nr, ni

    xr, xi = lax.fori_loop(0, tc, step, (st_ref[0], st_ref[1]), unroll=4)
    st_ref[0] = xr
    st_ref[1] = xi

    yy = jnp.dot(xs_ref[...].astype(BF16), cc_ref[...], preferred_element_type=F32)
    for s in range(n_slab):
        y_ref[s] = yy[:, s * LANES:(s + 1) * LANES]
    y = jnp.concatenate(
        [jnp.concatenate([y_ref[s, half_rows(b, s), :] for s in range(n_slab)], axis=1)
         for b in range(nb)], axis=0)
    y = y + d_ref[...] * u_ref[...].reshape(nb * tc, SSM_DIM)
    g = jax.nn.gelu(y)
    z = jnp.dot(g.astype(BF16), wg_ref[...], preferred_element_type=F32) + bg_ref[...]
    out = g * jax.nn.sigmoid(z)
    out = out * lax.rsqrt(jnp.mean(out * out, axis=-1, keepdims=True) + RMS_EPS) * gn_ref[...]
    o_ref[...] = out.reshape(nb, tc, SSM_DIM).astype(BF16)


def s5_group(u, lam_re, lam_im, log_dt, b_re, b_im, c_re, c_im, d_skip, w_glu, b_glu, gn):
    nb, seq, _ = u.shape
    assert 2 * nb == SUBLANES
    tc = S5_TC
    dt = jnp.exp(log_dt)[:, None]
    mag = jnp.exp(lam_re * dt)
    ang = lam_im * dt
    lb_re, lb_im = mag * jnp.cos(ang), mag * jnp.sin(ang)
    den = lam_re * lam_re + lam_im * lam_im
    n_re, n_im = lb_re - 1.0, lb_im
    f_re = (n_re * lam_re + n_im * lam_im) / den
    f_im = (n_im * lam_re - n_re * lam_im) / den
    bb_re = f_re[..., None] * b_re - f_im[..., None] * b_im
    bb_im = f_re[..., None] * b_im + f_im[..., None] * b_re
    gh = SSM_GROUPS // 2
    eye = jnp.eye(gh, dtype=F32)

    def in_mat(m):
        m = m.reshape(2, gh, SSM_STATE, SSM_GROUP_CH)
        bd = jnp.einsum('rgph,gk->rghkp', m, eye)
        return bd.reshape(SSM_DIM, S5_NH)

    def out_mat(m):
        m = m.reshape(2, gh, SSM_GROUP_CH, SSM_STATE)
        bd = jnp.einsum('rghp,gk->kprgh', m, eye)
        return bd.reshape(S5_NH, SSM_DIM)

    bb = jnp.concatenate([in_mat(bb_re), in_mat(bb_im)], axis=1).astype(BF16)
    cc = jnp.concatenate([out_mat(c_re), -out_mat(c_im)], axis=0).astype(BF16)

    def lane_vec(m):
        return jnp.tile(m.reshape(2, S5_NH), (nb, 1))

    a_re, a_im = lane_vec(lb_re), lane_vec(lb_im)


    rows = tc * 2 * nb
    out = pl.pallas_call(
        functools.partial(_s5_kernel, nb=nb),
        grid=(seq // tc,),
        in_specs=[
            pl.BlockSpec((nb, tc, SSM_DIM), lambda i: (0, i, 0)),
            _resident(bb.shape), _resident(cc.shape),
            _resident(a_re.shape), _resident(a_im.shape),
            _resident((1, SSM_DIM)), _resident((SSM_DIM, SSM_DIM)),
            _resident((1, SSM_DIM)), _resident((1, SSM_DIM)),
        ],
        out_specs=pl.BlockSpec((nb, tc, SSM_DIM), lambda i: (0, i, 0)),
        out_shape=jax.ShapeDtypeStruct((nb, seq, SSM_DIM), BF16),
        scratch_shapes=[pltpu.VMEM((2, 2 * nb, S5_NH), F32),
                        pltpu.VMEM((rows, 2 * S5_NH), F32),
                        pltpu.VMEM((SSM_DIM // LANES, rows, LANES), F32),
                        pltpu.VMEM((SSM_DIM // LANES, rows, LANES), F32)],
        compiler_params=_cparams(("arbitrary",)),
        name="s5",
    )(u, bb, cc, a_re, a_im, d_skip.reshape(1, -1), w_glu.astype(BF16),
      b_glu.reshape(1, -1), gn.reshape(1, -1))
    return out


def _conv_kernel(c_ref, w_ref, gn_ref, o_ref, zp_ref):
    tm = o_ref.shape[0]
    ch = c_ref[:, :CONV_DIM]
    gb = c_ref[:, CONV_DIM:2 * CONV_DIM]
    gc = c_ref[:, 2 * CONV_DIM:]

    @pl.when(pl.program_id(1) == 0)
    def _():
        zp_ref[pl.ds(0, SUBLANES), :] = jnp.zeros((SUBLANES, CONV_DIM), F32)

    @pl.when(pl.program_id(1) != 0)
    def _():
        zp_ref[pl.ds(0, SUBLANES), :] = zp_ref[pl.ds(tm, SUBLANES), :]

    zp_ref[pl.ds(SUBLANES, tm), :] = gc * ch
    acc = zp_ref[pl.ds(SUBLANES, tm), :] * w_ref[2:3, :]
    acc += zp_ref[pl.ds(SUBLANES - 1, tm), :] * w_ref[1:2, :]
    acc += zp_ref[pl.ds(SUBLANES - 2, tm), :] * w_ref[0:1, :]
    y = gb * acc
    y = y * lax.rsqrt(jnp.mean(y * y, axis=-1, keepdims=True) + RMS_EPS) * gn_ref[...]
    o_ref[...] = y.astype(BF16)


def conv_group(conv_in, conv_w, gn, nb, seq):
    tm = CONV_TM
    tiles = seq // tm
    return pl.pallas_call(
        _conv_kernel,
        grid=(nb, tiles),
        in_specs=[pl.BlockSpec((tm, 3 * CONV_DIM), lambda b, i: (b * tiles + i, 0)),
                  pl.BlockSpec((CONV_WIDTH, CONV_DIM), lambda b, i: (0, 0)),
                  pl.BlockSpec((1, CONV_DIM), lambda b, i: (0, 0))],
        out_specs=pl.BlockSpec((tm, CONV_DIM), lambda b, i: (b * tiles + i, 0)),
        out_shape=jax.ShapeDtypeStruct((nb * seq, CONV_DIM), BF16),
        scratch_shapes=[pltpu.VMEM((tm + SUBLANES, CONV_DIM), F32)],
        compiler_params=_cparams(("arbitrary", "arbitrary")),
        name="conv",
    )(conv_in, conv_w, gn.reshape(1, -1))


def _key_to_f32(key):
    bits = key ^ ((key >> 31) & jnp.int32(0x7FFFFFFF))
    return pltpu.bitcast(bits, F32)


def _attn_kernel(q_ref, qi_ref, wi_ref, k_ref, vt_ref, kk_ref, gn_ref, o_ref,
                 sc_ref, lhs_ref, wib_ref, m_ref, l_ref, a_ref, acc_ref, s_ref, p_ref, *, topk, n_tiles):
    tq, kb_sz = ATT_TQ, ATT_KB
    per_tile = tq // kb_sz
    i = pl.program_id(1)
    has_score = i < n_tiles
    has_att = i > 0
    n_kb = (i + 1) * per_tile
    n_kb_att = i * per_tile
    cur = i % 2
    prv = 1 - cur
    lane128 = lax.broadcasted_iota(jnp.int32, (tq, LANES), 1)

    for h in range(IDX_HEADS):
        pair = qi_ref[:, (h // 2) * LANES:(h // 2 + 1) * LANES]
        mine = (lane128 // IDX_DIM) == (h % 2)
        lhs_ref[h] = jnp.where(mine, pair, jnp.zeros_like(pair))
        wib_ref[h] = jnp.broadcast_to(wi_ref[:, h:h + 1], (tq, LANES))

    key_row = lax.broadcasted_iota(jnp.int32, (kb_sz, tq), 0)
    qry_lane = lax.broadcasted_iota(jnp.int32, (kb_sz, tq), 1)
    key_chunk = key_row // CHUNK
    qry_chunk = qry_lane // CHUNK + i * (tq // CHUNK)

    def admissible(kb):
        return key_chunk + kb * (kb_sz // CHUNK) <= qry_chunk

    def score_block(kb):
        k0 = pl.multiple_of(kb * kb_sz, kb_sz)
        kk = kk_ref[pl.ds(k0, kb_sz), :]
        acc = jnp.zeros((tq, kb_sz), F32)
        for h in range(IDX_HEADS):
            logit = lax.dot_general(lhs_ref[h], kk, (((1,), (1,)), ((), ())),
                                    preferred_element_type=F32)
            w = wib_ref[h]
            acc = acc + jnp.maximum(logit, 0.0) * jnp.concatenate([w] * (kb_sz // LANES), axis=1)
        sc_ref[cur, kb] = jnp.where(admissible(kb), acc.T, -jnp.inf)

    def att_logits(kb):
        k0 = pl.multiple_of(kb * kb_sz, kb_sz)
        bias = sc_ref[prv, kb]
        for h in range(N_HEADS):
            hs = slice(h * HEAD_DIM, (h + 1) * HEAD_DIM)
            s_ref[h] = lax.dot_general(k_ref[pl.ds(k0, kb_sz), hs], q_ref[:, hs],
                                       (((1,), (1,)), ((), ())),
                                       preferred_element_type=F32) + bias

    def att_update(kb):
        for h in range(N_HEADS):
            s = s_ref[h]
            m_old = m_ref[h]
            m_new = jnp.maximum(m_old, jnp.max(s, axis=0, keepdims=True))
            p = jnp.exp2(s - m_new)
            p_ref[h] = p.astype(BF16)
            a_ref[h] = jnp.exp2(m_old - m_new)
            m_ref[h] = m_new
        ones = jnp.ones((2 * SUBLANES, kb_sz), BF16)
        for h in range(N_HEADS):
            hs = slice(h * HEAD_DIM, (h + 1) * HEAD_DIM)
            pv = jnp.dot(jnp.concatenate([vt_ref[kb, hs, :], ones], axis=0), p_ref[h],
                         preferred_element_type=F32)
            acc_ref[h] = a_ref[h] * acc_ref[h] + pv[:HEAD_DIM]
            l_ref[h] = a_ref[h] * l_ref[h] + pv[HEAD_DIM:HEAD_DIM + 1]

    m_ref[...] = jnp.full(m_ref.shape, NEG_BIG, F32)
    l_ref[...] = jnp.zeros(l_ref.shape, F32)
    acc_ref[...] = jnp.zeros(acc_ref.shape, F32)

    def both(kb, _):
        att_logits(kb)
        score_block(kb)
        att_update(kb)
        return 0

    def only_score(kb, _):
        score_block(kb)
        return 0

    def only_att(kb, _):
        att_logits(kb)
        att_update(kb)
        return 0

    @pl.when(has_score)
    def _():
        lax.fori_loop(0, n_kb_att, both, 0)
        lax.fori_loop(n_kb_att, n_kb, only_score, 0)

    @pl.when(jnp.logical_not(has_score))
    def _():
        lax.fori_loop(0, n_kb_att, only_att, 0)

    @pl.when(has_att)
    def _():
        ssq = jnp.zeros((1, tq), F32)
        for h in range(N_HEADS):
            yh = acc_ref[h] / l_ref[h]
            acc_ref[h] = yh
            ssq = ssq + jnp.sum(yh * yh, axis=0, keepdims=True)
        scale = lax.rsqrt(ssq / ATTN_DIM + RMS_EPS)
        for h in range(N_HEADS):
            hs = slice(h * HEAD_DIM, (h + 1) * HEAD_DIM)
            o_ref[:, hs] = ((acc_ref[h] * scale).T * gn_ref[:, hs]).astype(BF16)

    @pl.when(has_score)
    def _():
        _select_keys(sc_ref, cur, i, n_kb, admissible, key_row, topk)


def _select_keys(sc_ref, cur, i, n_kb, admissible, key_row, topk):
    tq, kb_sz = ATT_TQ, ATT_KB

    @pl.when(n_kb % 2 == 1)
    def _():
        sc_ref[cur, n_kb] = jnp.full((kb_sz, tq), -jnp.inf, F32)

    def count_if(pred):
        def one(kb, cnt):
            hit = jnp.where(pred(sc_ref[cur, kb], kb * kb_sz), 1.0, 0.0)
            return cnt + jnp.sum(hit.reshape(kb_sz // SUBLANES, SUBLANES, tq), axis=0)

        def pair(j, cnt):
            return one(2 * j + 1, one(2 * j, cnt))

        cnt = lax.fori_loop(0, (n_kb + 1) // 2, pair, jnp.zeros((SUBLANES, tq), F32))
        return jnp.sum(cnt, axis=0, keepdims=True)

    k_f = jnp.float32(topk)
    q_lane = lax.broadcasted_iota(jnp.int32, (1, tq), 1)
    n_adm = ((i * (tq // CHUNK) + q_lane // CHUNK + 1) * CHUNK).astype(F32)
    searched = n_adm > k_f
    c0 = count_if(lambda s, _: s >= 0.0)
    pos = c0 >= k_f
    thr0 = jnp.where(pos, jnp.int32(0), jnp.int32(-2 ** 31))
    cnt0 = jnp.where(pos, c0, (n_kb * kb_sz).astype(F32))

    def unresolved(cnt_thr):
        return jnp.max(jnp.where(searched, cnt_thr, k_f)) > k_f

    def bisect_cond(st):
        b, _, cnt_thr = st
        return jnp.logical_and(b < 31, unresolved(cnt_thr))

    def bisect_step(b, thr, cnt_thr):
        cand = thr + (jnp.int32(1) << jnp.maximum(30 - b, 0))
        cand_f = _key_to_f32(cand)
        c = count_if(lambda s, _: s >= cand_f)
        ok = jnp.logical_and(c >= k_f, b < 31)
        return jnp.where(ok, cand, thr), jnp.where(ok, c, cnt_thr)

    def bisect(st):
        b, thr, cnt_thr = st
        thr, cnt_thr = bisect_step(b, thr, cnt_thr)
        thr, cnt_thr = bisect_step(b + 1, thr, cnt_thr)
        return b + 2, thr, cnt_thr

    _, thr, cnt_thr = lax.while_loop(bisect_cond, bisect, (jnp.int32(0), thr0, cnt0))
    thr_f = jnp.where(searched, _key_to_f32(thr), -jnp.inf)

    def tie_cut(_):
        need = k_f - count_if(lambda s, _: s > thr_f)
        n_bits = (sc_ref.shape[1] * kb_sz - 1).bit_length()

        def body(b, m):
            step = jnp.int32(1) << (n_bits - 1 - b)
            top = m + step - 1
            c = count_if(lambda s, k0: jnp.logical_and(s == thr_f, key_row + k0 <= top))
            return jnp.where(c < need, m + step, m)

        return lax.fori_loop(0, n_bits, body, jnp.zeros((1, tq), jnp.int32))

    idx_cut = lax.cond(unresolved(cnt_thr), tie_cut,
                       lambda _: jnp.full((1, tq), 2 ** 30, jnp.int32), 0)

    def bias_block(kb, _):
        s = sc_ref[cur, kb]
        sel = jnp.logical_or(s > thr_f, jnp.logical_and(s == thr_f, key_row + kb * kb_sz <= idx_cut))
        sel = jnp.logical_and(sel, admissible(kb))
        sc_ref[cur, kb] = jnp.where(sel, 0.0, -jnp.inf)
        return 0

    lax.fori_loop(0, n_kb, bias_block, 0)


def attn_group(q, k, vt, qi, kk, wi, gn, nb, seq):
    tq = ATT_TQ
    tiles = seq // tq
    topk = min(TOPK_MAX, seq // 4)
    assert topk <= ATT_KB and ATT_TQ % ATT_KB == 0 and ATT_KB % CHUNK == 0 and (seq // ATT_KB) % 2 == 0
    att_row = lambda w: pl.BlockSpec((tq, w), lambda b, i: (b * tiles + jnp.maximum(i - 1, 0), 0))
    sc_row = lambda w: pl.BlockSpec((tq, w), lambda b, i: (b * tiles + jnp.minimum(i, tiles - 1), 0))
    whole = lambda w: pl.BlockSpec((seq, w), lambda b, i: (b, 0), pipeline_mode=pl.Buffered(1))
    vt_spec = pl.BlockSpec((None, seq // ATT_KB, ATTN_DIM, ATT_KB), lambda b, i: (b, 0, 0, 0),
                           pipeline_mode=pl.Buffered(1))
    return pl.pallas_call(
        functools.partial(_attn_kernel, topk=topk, n_tiles=tiles),
        grid=(nb, tiles + 1),
        in_specs=[att_row(ATTN_DIM), sc_row(IDX_HEADS * IDX_DIM), sc_row(LANES),
                  whole(ATTN_DIM), vt_spec, whole(LANES),
                  pl.BlockSpec((1, ATTN_DIM), lambda b, i: (0, 0))],
        out_specs=att_row(ATTN_DIM),
        out_shape=jax.ShapeDtypeStruct((nb * seq, ATTN_DIM), BF16),
        scratch_shapes=[pltpu.VMEM((2, seq // ATT_KB, ATT_KB, tq), F32),
                        pltpu.VMEM((IDX_HEADS, tq, LANES), BF16),
                        pltpu.VMEM((IDX_HEADS, tq, LANES), F32),
                        pltpu.VMEM((N_HEADS, 1, tq), F32),
                        pltpu.VMEM((N_HEADS, 1, tq), F32),
                        pltpu.VMEM((N_HEADS, 1, tq), F32),
                        pltpu.VMEM((N_HEADS, HEAD_DIM, tq), F32),
                        pltpu.VMEM((N_HEADS, ATT_KB, tq), F32),
                        pltpu.VMEM((N_HEADS, ATT_KB, tq), BF16)],
        compiler_params=_cparams(("arbitrary", "arbitrary"), VMEM_LIMIT_ATTN),
        name="attn",
    )(q, qi, wi, k, vt, kk, gn.reshape(1, -1))


def _layer_norm(r, g, b):
    mu = jnp.mean(r, axis=-1, keepdims=True)
    d = r - mu
    var = jnp.mean(d * d, axis=-1, keepdims=True)
    return d * lax.rsqrt(var + LN_EPS) * g + b


def _out_proj_kernel(x_ref, ys_ref, yc_ref, ya_ref, w_ref, gate_ref, g_ref, b_ref, o_ref):
    mix = jnp.dot(ys_ref[...], w_ref[pl.ds(0, SSM_DIM), :], preferred_element_type=F32)
    mix += jnp.dot(yc_ref[...], w_ref[pl.ds(SSM_DIM, CONV_DIM), :], preferred_element_type=F32)
    mix += jnp.dot(ya_ref[...], w_ref[pl.ds(SSM_DIM + CONV_DIM, ATTN_DIM), :],
                   preferred_element_type=F32)
    r = ALPHA * x_ref[...] + gate_ref[0] * mix
    o_ref[...] = _layer_norm(r, g_ref[...], b_ref[...])


def out_proj(x, ys, yc, ya, w_o, gate, ln_g, ln_b):
    nb, seq, _ = x.shape
    tm = OUT_TM
    tiles = seq // tm
    row = lambda w: pl.BlockSpec((tm, w), lambda i: (i, 0))
    xrow = pl.BlockSpec((None, tm, D_MODEL), lambda i: (i // tiles, i % tiles, 0))
    vec = pl.BlockSpec((1, D_MODEL), lambda i: (0, 0))
    return pl.pallas_call(
        _out_proj_kernel,
        grid=(nb * tiles,),
        in_specs=[xrow, row(SSM_DIM), row(CONV_DIM), row(ATTN_DIM),
                  _resident((D_MODEL, D_MODEL)),
                  pl.BlockSpec((1, 1, D_MODEL), lambda i: (i // tiles, 0, 0)), vec, vec],
        out_specs=xrow,
        out_shape=jax.ShapeDtypeStruct((nb, seq, D_MODEL), F32),
        compiler_params=_cparams(("arbitrary",)),
        name="out_proj",
    )(x, ys, yc, ya, w_o.astype(BF16), gate, ln_g.reshape(1, -1), ln_b.reshape(1, -1))


def _ffn_kernel(x_ref, sc_ref, sh_ref, w1_ref, w2_ref, gate_ref, g_ref, b_ref, o_ref,
                h_ref, acc_ref):
    j = pl.program_id(1)

    @pl.when(j == 0)
    def _():
        h_ref[...] = (x_ref[...] * (1.0 + sc_ref[0]) + sh_ref[0]).astype(BF16)
        acc_ref[...] = jnp.zeros_like(acc_ref)

    a = jnp.maximum(jnp.dot(h_ref[...], w1_ref[...], preferred_element_type=F32), 0.0)
    a = (a * a).astype(BF16)
    for c in range(D_MODEL // FFN_TN):
        cs = slice(c * FFN_TN, (c + 1) * FFN_TN)
        acc_ref[:, cs] += jnp.dot(a, w2_ref[:, cs], preferred_element_type=F32)

    @pl.when(j == pl.num_programs(1) - 1)
    def _():
        r = ALPHA * x_ref[...] + gate_ref[0] * acc_ref[...]
        o_ref[...] = _layer_norm(r, g_ref[...], b_ref[...])


def ffn(x, sc, sh, w1_bf16, w2_bf16, layer, gate, ln_g, ln_b):
    nb, seq, _ = x.shape
    tm, tf = FFN_TM, FFN_TF
    tiles = seq // tm
    row = pl.BlockSpec((None, tm, D_MODEL), lambda i, j: (i // tiles, i % tiles, 0))
    mod = pl.BlockSpec((1, 1, D_MODEL), lambda i, j: (i // tiles, 0, 0))
    vec = pl.BlockSpec((1, D_MODEL), lambda i, j: (0, 0))
    return pl.pallas_call(
        _ffn_kernel,
        grid=(nb * tiles, D_FF // tf),
        in_specs=[row, mod, mod,
                  pl.BlockSpec((None, D_MODEL, tf), lambda i, j: (layer, 0, j)),
                  pl.BlockSpec((None, tf, D_MODEL), lambda i, j: (layer, j, 0)),
                  mod, vec, vec],
        out_specs=row,
        out_shape=jax.ShapeDtypeStruct((nb, seq, D_MODEL), F32),
        scratch_shapes=[pltpu.VMEM((tm, D_MODEL), BF16), pltpu.VMEM((tm, D_MODEL), F32)],
        compiler_params=_cparams(("arbitrary", "arbitrary")),
        name="ffn",
    )(x, sc, sh, w1_bf16, w2_bf16, gate, ln_g.reshape(1, -1), ln_b.reshape(1, -1))


def kernel(x, c, w_ada, b_ada, w_in, lam_re, lam_im, log_dt, ssm_b_re, ssm_b_im, ssm_c_re, ssm_c_im,
           ssm_d, w_glu, b_glu, conv_w, gnorm_g, w_o, ln1_g, ln1_b, w_ff1, w_ff2, ln2_g, ln2_b):
    nb, seq, _ = x.shape
    tables = rope_tables(seq)
    mods = adaln(c, w_ada, b_ada)
    w_in_bf16 = w_in.astype(BF16)
    w1_bf16, w2_bf16 = w_ff1.astype(BF16), w_ff2.astype(BF16)
    for l in range(DEPTH):
        sh1, sc1, g1, sh2, sc2, g2 = (m[:, None, :] for m in jnp.split(mods[l], 6, axis=-1))
        u, conv_in, q, k, vt, qi, kk, wi = in_proj(x, sc1, sh1, w_in_bf16, l, tables)
        gn = gnorm_g[l]
        ys = s5_group(u.reshape(nb, seq, SSM_DIM), lam_re[l], lam_im[l], log_dt[l], ssm_b_re[l],
                      ssm_b_im[l], ssm_c_re[l], ssm_c_im[l], ssm_d[l], w_glu[l], b_glu[l],
                      gn[:SSM_DIM]).reshape(nb * seq, SSM_DIM)
        yc = conv_group(conv_in, conv_w[l], gn[SSM_DIM:SSM_DIM + CONV_DIM], nb, seq)
        ya = attn_group(q, k, vt, qi, kk, wi, gn[SSM_DIM + CONV_DIM:], nb, seq)
        x = out_proj(x, ys, yc, ya, w_o[l], g1, ln1_g[l], ln1_b[l])
        x = ffn(x, sc2, sh2, w1_bf16, w2_bf16, l, g2, ln2_g[l], ln2_b[l])
    return x
```

```python
import functools
import math

import jax
import jax.numpy as jnp
from jax import lax
from jax.experimental import pallas as pl
from jax.experimental.pallas import tpu as pltpu

D_MODEL = 2048
DEPTH = 2
CHUNK = 64
SSM_DIM = 512
SSM_GROUP_CH = 16
SSM_GROUPS = 32
SSM_STATE = 64
CONV_DIM = 512
CONV_WIDTH = 3
ATTN_DIM = 1024
HEAD_DIM = 128
N_HEADS = 8
IDX_HEADS = 16
IDX_DIM = 64
TOPK_MAX = 256
D_FF = 4 * D_MODEL
ROPE_THETA = 10000.0
ALPHA = (2.0 * DEPTH) ** 0.25
LN_EPS = 1e-5
RMS_EPS = 1e-6

_in_edges = [0, SSM_DIM, SSM_DIM + 3 * CONV_DIM]
_in_edges += [_in_edges[-1] + ATTN_DIM, _in_edges[-1] + 2 * ATTN_DIM, _in_edges[-1] + 3 * ATTN_DIM]
_in_edges += [_in_edges[-1] + IDX_HEADS * IDX_DIM]
_in_edges += [_in_edges[-1] + IDX_DIM + IDX_HEADS]
IN_COLS = {name: (_in_edges[j], _in_edges[j + 1])
           for j, name in enumerate(("u", "conv", "q", "k", "v", "qi", "tail"))}

LANES = 128
SUBLANES = 8
VMEM_CAPACITY = 64 * 1024 * 1024
VMEM_LIMIT = 56 * 1024 * 1024
VMEM_LIMIT_ATTN = VMEM_CAPACITY - 4 * 1024 * 1024

ADA_TN = 1024
PROJ_TM = 256
S5_TC = 128
S5_HALF = SSM_DIM // 2
S5_NH = SSM_GROUPS // 2 * SSM_STATE
ATT_TQ = 512
ATT_KB = 256
OUT_TM = 512
FFN_TM = 512
FFN_TF = 1024
FFN_TN = 512

BF16 = jnp.bfloat16
F32 = jnp.float32
NEG_BIG = -1e30
Q_SCALE = HEAD_DIM ** -0.5 * math.log2(math.e)


def _cparams(sem, vmem_limit=VMEM_LIMIT):
    return pltpu.CompilerParams(dimension_semantics=sem, vmem_limit_bytes=vmem_limit)


def _resident(shape):
    nd = len(shape)
    return pl.BlockSpec(shape, lambda *_: (0,) * nd, pipeline_mode=pl.Buffered(1))


def _adaln_kernel(c_ref, w_ref, b_ref, o_ref):
    w = w_ref[0].astype(BF16)
    o_ref[0] = jnp.dot(c_ref[...], w, preferred_element_type=F32) + b_ref[0]


def adaln(c, w_ada, b_ada):
    nb = c.shape[0]
    rows = 16
    cp = jnp.zeros((rows, D_MODEL), BF16).at[:nb].set(c.astype(BF16))
    n_out = w_ada.shape[-1]
    out = pl.pallas_call(
        _adaln_kernel,
        grid=(DEPTH, n_out // ADA_TN),
        in_specs=[
            pl.BlockSpec((rows, D_MODEL), lambda l, j: (0, 0)),
            pl.BlockSpec((1, D_MODEL, ADA_TN), lambda l, j: (l, 0, j)),
            pl.BlockSpec((1, 1, ADA_TN), lambda l, j: (l, 0, j)),
        ],
        out_specs=pl.BlockSpec((1, rows, ADA_TN), lambda l, j: (l, 0, j)),
        out_shape=jax.ShapeDtypeStruct((DEPTH, rows, n_out), F32),
        compiler_params=_cparams(("arbitrary", "arbitrary")),
        name="adaln",
    )(cp, w_ada, b_ada.reshape(DEPTH, 1, n_out))
    return out[:, :nb]


def _rope_halves(x, cos, sin_signed, half):
    if 2 * half == LANES:
        swapped = pltpu.roll(x, half, axis=1)
    else:
        lane = lax.broadcasted_iota(jnp.int32, x.shape, 1)
        first = (lane % (2 * half)) < half
        swapped = jnp.where(first, pltpu.roll(x, LANES - half, axis=1), pltpu.roll(x, half, axis=1))
    return x * cos + swapped * sin_signed


def _short_conv(c, w_ref, gn_ref, o_ref, zp_ref, *, first):
    tm = c.shape[0]
    ch, gb, gc = c[:, :CONV_DIM], c[:, CONV_DIM:2 * CONV_DIM], c[:, 2 * CONV_DIM:]

    @pl.when(first)
    def _():
        zp_ref[pl.ds(0, SUBLANES), :] = jnp.zeros((SUBLANES, CONV_DIM), F32)

    @pl.when(jnp.logical_not(first))
    def _():
        zp_ref[pl.ds(0, SUBLANES), :] = zp_ref[pl.ds(tm, SUBLANES), :]

    zp_ref[pl.ds(SUBLANES, tm), :] = gc * ch
    acc = zp_ref[pl.ds(SUBLANES, tm), :] * w_ref[2:3, :]
    acc += zp_ref[pl.ds(SUBLANES - 1, tm), :] * w_ref[1:2, :]
    acc += zp_ref[pl.ds(SUBLANES - 2, tm), :] * w_ref[0:1, :]
    y = gb * acc
    y = y * lax.rsqrt(jnp.mean(y * y, axis=-1, keepdims=True) + RMS_EPS) * gn_ref[...]
    o_ref[...] = y.astype(BF16)


def _in_proj_kernel(x_ref, sc_ref, sh_ref, w_ref, wv_ref, wt_ref,
                    cosa_ref, sina_ref, cosi_ref, sini_ref, cw_ref, cg_ref,
                    u_ref, yc_ref, q_ref, k_ref, vt_ref, qi_ref, kk_ref, wi_ref, zp_ref,
                    *, tiles_per_seq):
    def proj(name):
        lo, hi = IN_COLS[name]
        return jnp.dot(h, w_ref[:, lo:hi], preferred_element_type=F32)

    h = (x_ref[...] * (1.0 + sc_ref[0]) + sh_ref[0]).astype(BF16)
    u_ref[...] = proj("u")
    _short_conv(proj("conv"), cw_ref, cg_ref, yc_ref, zp_ref,
                first=pl.program_id(0) % tiles_per_seq == 0)
    vt_ref[...] = lax.dot_general(wv_ref[...], h, (((1,), (1,)), ((), ())),
                                  preferred_element_type=F32).astype(BF16)

    cosa, sina = cosa_ref[...], sina_ref[...]
    q = proj("q")
    k = proj("k")
    for hd in range(N_HEADS):
        sl = slice(hd * HEAD_DIM, (hd + 1) * HEAD_DIM)
        q_ref[:, sl] = (_rope_halves(q[:, sl], cosa, sina, HEAD_DIM // 2)
                        * Q_SCALE).astype(BF16)
        k_ref[:, sl] = _rope_halves(k[:, sl], cosa, sina, HEAD_DIM // 2).astype(BF16)

    cosi, sini = cosi_ref[...], sini_ref[...]
    qi = proj("qi")
    for g in range(IDX_HEADS * IDX_DIM // LANES):
        sl = slice(g * LANES, (g + 1) * LANES)
        qi_ref[:, sl] = _rope_halves(qi[:, sl], cosi, sini, IDX_DIM // 2).astype(BF16)

    tail = jnp.dot(h, wt_ref[...], preferred_element_type=F32)
    ki2 = jnp.where(lax.broadcasted_iota(jnp.int32, tail.shape, 1) < IDX_DIM,
                    tail, pltpu.roll(tail, IDX_DIM, axis=1))
    kk_ref[...] = _rope_halves(ki2, cosi, sini, IDX_DIM // 2).astype(BF16)
    wi_ref[...] = pltpu.roll(tail, LANES - IDX_DIM, axis=1) * (
        (IDX_DIM ** -0.5) * (IDX_HEADS ** -0.5))


def in_proj(x, sc, sh, w_in_bf16, layer, tables, conv_w, gn_conv):
    nb, seq, _ = x.shape
    n = nb * seq
    tm = PROJ_TM
    assert tm == ATT_KB
    tiles_per_seq = seq // tm
    lo, hi = IN_COLS["v"]
    wv = w_in_bf16[layer, :, lo:hi].T
    lo, hi = IN_COLS["tail"]
    wt = jnp.zeros((D_MODEL, LANES), BF16).at[:, :hi - lo].set(w_in_bf16[layer, :, lo:hi])
    main_cols = IN_COLS["tail"][0]
    cosa, sina, cosi, sini = tables

    row = lambda w: pl.BlockSpec((tm, w), lambda i: (i, 0))
    mod = pl.BlockSpec((1, 1, D_MODEL), lambda i: (i // tiles_per_seq, 0, 0))
    tab = pl.BlockSpec((tm, LANES), lambda i: (i % tiles_per_seq, 0))
    outs = pl.pallas_call(
        functools.partial(_in_proj_kernel, tiles_per_seq=tiles_per_seq),
        grid=(n // tm,),
        in_specs=[pl.BlockSpec((None, tm, D_MODEL), lambda i: (i // tiles_per_seq, i % tiles_per_seq, 0)),
                  mod, mod,
                  pl.BlockSpec((None, D_MODEL, main_cols), lambda i: (layer, 0, 0),
                               pipeline_mode=pl.Buffered(1)),
                  _resident(wv.shape), _resident(wt.shape),
                  tab, tab, tab, tab,
                  _resident((CONV_WIDTH, CONV_DIM)), _resident((1, CONV_DIM))],
        out_specs=[row(SSM_DIM), row(CONV_DIM), row(ATTN_DIM), row(ATTN_DIM),
                   pl.BlockSpec((None, None, ATTN_DIM, tm),
                                lambda i: (i // tiles_per_seq, i % tiles_per_seq, 0, 0)),
                   row(IDX_HEADS * IDX_DIM), row(LANES), row(LANES)],
        out_shape=[jax.ShapeDtypeStruct((n, SSM_DIM), F32),
                   jax.ShapeDtypeStruct((n, CONV_DIM), BF16),
                   jax.ShapeDtypeStruct((n, ATTN_DIM), BF16),
                   jax.ShapeDtypeStruct((n, ATTN_DIM), BF16),
                   jax.ShapeDtypeStruct((nb, tiles_per_seq, ATTN_DIM, tm), BF16),
                   jax.ShapeDtypeStruct((n, IDX_HEADS * IDX_DIM), BF16),
                   jax.ShapeDtypeStruct((n, LANES), BF16),
                   jax.ShapeDtypeStruct((n, LANES), F32)],
        scratch_shapes=[pltpu.VMEM((tm + SUBLANES, CONV_DIM), F32)],
        compiler_params=_cparams(("arbitrary",)),
        name="in_proj",
    )(x, sc, sh, w_in_bf16, wv, wt, cosa, sina, cosi, sini, conv_w, gn_conv.reshape(1, -1))
    return outs


def rope_tables(seq):
    def tab(dim):
        inv = 1.0 / (ROPE_THETA ** (jnp.arange(0, dim, 2, dtype=F32) / dim))
        ang = jnp.arange(seq, dtype=F32)[:, None] * inv[None, :]
        cos, sin = jnp.cos(ang), jnp.sin(ang)
        reps = LANES // dim
        return (jnp.tile(jnp.concatenate([cos, cos], axis=1), (1, reps)),
                jnp.tile(jnp.concatenate([-sin, sin], axis=1), (1, reps)))
    cosa, sina = tab(HEAD_DIM)
    cosi, sini = tab(IDX_DIM)
    return cosa, sina, cosi, sini


def _s5_kernel(u_ref, bb_ref, cc_ref, are_ref, aim_ref, d_ref, wg_ref, bg_ref, gn_ref,
               o_ref, st_ref, xs_ref, y_ref, lhs_ref, *, nb):
    rows_t = 2 * nb
    tc = xs_ref.shape[0] // rows_t
    n_slab = SSM_DIM // LANES

    def half_rows(b, s):
        return pl.ds(2 * b + s // (n_slab // 2), tc, stride=rows_t)

    @pl.when(pl.program_id(0) == 0)
    def _():
        st_ref[...] = jnp.zeros_like(st_ref)
        lhs_ref[...] = jnp.zeros_like(lhs_ref)

    for b in range(nb):
        for s in range(n_slab):
            lhs_ref[s, half_rows(b, s), :] = u_ref[b, :, s * LANES:(s + 1) * LANES]
    lhs = jnp.concatenate([lhs_ref[s] for s in range(n_slab)], axis=1).astype(BF16)
    xs_ref[...] = jnp.dot(lhs, bb_ref[...], preferred_element_type=F32)
    a_re, a_im = are_ref[...], aim_ref[...]

    def step(t, carry):
        xr, xi = carry
        r0 = pl.multiple_of(t * rows_t, rows_t)
        bur = xs_ref[pl.ds(r0, rows_t), :S5_NH]
        bui = xs_ref[pl.ds(r0, rows_t), S5_NH:]
        nr = a_re * xr - a_im * xi + bur
        ni = a_re * xi + a_im * xr + bui
        xs_ref[pl.ds(r0, rows_t), :S5_NH] = nr
        xs_ref[pl.ds(r0, rows_t), S5_NH:] = ni
        return nr, ni

    xr, xi = lax.fori_loop(0, tc, step, (st_ref[0], st_ref[1]), unroll=4)
    st_ref[0] = xr
    st_ref[1] = xi

    yy = jnp.dot(xs_ref[...].astype(BF16), cc_ref[...], preferred_element_type=F32)
    for s in range(n_slab):
        y_ref[s] = yy[:, s * LANES:(s + 1) * LANES]
    y = jnp.concatenate(
        [jnp.concatenate([y_ref[s, half_rows(b, s), :] for s in range(n_slab)], axis=1)
         for b in range(nb)], axis=0)
    y = y + d_ref[...] * u_ref[...].reshape(nb * tc, SSM_DIM)
    g = jax.nn.gelu(y)
    z = jnp.dot(g.astype(BF16), wg_ref[...], preferred_element_type=F32) + bg_ref[...]
    out = g * jax.nn.sigmoid(z)
    out = out * lax.rsqrt(jnp.mean(out * out, axis=-1, keepdims=True) + RMS_EPS) * gn_ref[...]
    o_ref[...] = out.reshape(nb, tc, SSM_DIM).astype(BF16)


def s5_group(u, lam_re, lam_im, log_dt, b_re, b_im, c_re, c_im, d_skip, w_glu, b_glu, gn):
    nb, seq, _ = u.shape
    assert 2 * nb == SUBLANES
    tc = S5_TC
    dt = jnp.exp(log_dt)[:, None]
    mag = jnp.exp(lam_re * dt)
    ang = lam_im * dt
    lb_re, lb_im = mag * jnp.cos(ang), mag * jnp.sin(ang)
    den = lam_re * lam_re + lam_im * lam_im
    n_re, n_im = lb_re - 1.0, lb_im
    f_re = (n_re * lam_re + n_im * lam_im) / den
    f_im = (n_im * lam_re - n_re * lam_im) / den
    bb_re = f_re[..., None] * b_re - f_im[..., None] * b_im
    bb_im = f_re[..., None] * b_im + f_im[..., None] * b_re
    gh = SSM_GROUPS // 2
    eye = jnp.eye(gh, dtype=F32)

    def in_mat(m):
        m = m.reshape(2, gh, SSM_STATE, SSM_GROUP_CH)
        bd = jnp.einsum('rgph,gk->rghkp', m, eye)
        return bd.reshape(SSM_DIM, S5_NH)

    def out_mat(m):
        m = m.reshape(2, gh, SSM_GROUP_CH, SSM_STATE)
        bd = jnp.einsum('rghp,gk->kprgh', m, eye)
        return bd.reshape(S5_NH, SSM_DIM)

    bb = jnp.concatenate([in_mat(bb_re), in_mat(bb_im)], axis=1).astype(BF16)
    cc = jnp.concatenate([out_mat(c_re), -out_mat(c_im)], axis=0).astype(BF16)

    def lane_vec(m):
        return jnp.tile(m.reshape(2, S5_NH), (nb, 1))

    a_re, a_im = lane_vec(lb_re), lane_vec(lb_im)


    rows = tc * 2 * nb
    out = pl.pallas_call(
        functools.partial(_s5_kernel, nb=nb),
        grid=(seq // tc,),
        in_specs=[
            pl.BlockSpec((nb, tc, SSM_DIM), lambda i: (0, i, 0)),
            _resident(bb.shape), _resident(cc.shape),
            _resident(a_re.shape), _resident(a_im.shape),
            _resident((1, SSM_DIM)), _resident((SSM_DIM, SSM_DIM)),
            _resident((1, SSM_DIM)), _resident((1, SSM_DIM)),
        ],
        out_specs=pl.BlockSpec((nb, tc, SSM_DIM), lambda i: (0, i, 0)),
        out_shape=jax.ShapeDtypeStruct((nb, seq, SSM_DIM), BF16),
        scratch_shapes=[pltpu.VMEM((2, 2 * nb, S5_NH), F32),
                        pltpu.VMEM((rows, 2 * S5_NH), F32),
                        pltpu.VMEM((SSM_DIM // LANES, rows, LANES), F32),
                        pltpu.VMEM((SSM_DIM // LANES, rows, LANES), F32)],
        compiler_params=_cparams(("arbitrary",)),
        name="s5",
    )(u, bb, cc, a_re, a_im, d_skip.reshape(1, -1), w_glu.astype(BF16),
      b_glu.reshape(1, -1), gn.reshape(1, -1))
    return out


def _key_to_f32(key):
    bits = key ^ ((key >> 31) & jnp.int32(0x7FFFFFFF))
    return pltpu.bitcast(bits, F32)


def _attn_kernel(q_ref, qi_ref, wi_ref, k_ref, vt_ref, kk_ref, gn_ref, o_ref,
                 sc_ref, wib_ref, m_ref, l_ref, a_ref, acc_ref, s_ref, p_ref, *, topk, n_tiles):
    tq, kb_sz = ATT_TQ, ATT_KB
    per_tile = tq // kb_sz
    i = pl.program_id(1)
    has_score = i < n_tiles
    has_att = i > 0
    n_kb = (i + 1) * per_tile
    n_kb_att = i * per_tile
    cur = i % 2
    prv = 1 - cur

    for h in range(IDX_HEADS):
        wib_ref[h] = jnp.broadcast_to(wi_ref[:, h:h + 1], (tq, LANES))

    key_row = lax.broadcasted_iota(jnp.int32, (kb_sz, tq), 0)
    qry_lane = lax.broadcasted_iota(jnp.int32, (kb_sz, tq), 1)
    key_chunk = key_row // CHUNK
    qry_chunk = qry_lane // CHUNK + i * (tq // CHUNK)

    def admissible(kb):
        return key_chunk + kb * (kb_sz // CHUNK) <= qry_chunk

    def score_block(kb):
        k0 = pl.multiple_of(kb * kb_sz, kb_sz)
        kk = kk_ref[pl.ds(k0, kb_sz), :]
        lane = lax.broadcasted_iota(jnp.int32, kk.shape, 1)
        kk_half = [jnp.where((lane // IDX_DIM) == half, kk, jnp.zeros_like(kk)) for half in range(2)]
        acc = jnp.zeros((tq, kb_sz), F32)
        for h in list(range(0, IDX_HEADS, 2)) + list(range(1, IDX_HEADS, 2)):
            pair = qi_ref[:, (h // 2) * LANES:(h // 2 + 1) * LANES]
            logit = lax.dot_general(pair, kk_half[h % 2], (((1,), (1,)), ((), ())),
                                    preferred_element_type=F32)
            w = wib_ref[h]
            acc = acc + jnp.maximum(logit, 0.0) * jnp.concatenate([w] * (kb_sz // LANES), axis=1)
        sc_ref[cur, kb] = jnp.where(admissible(kb), acc.T, -jnp.inf)

    def att_logits(kb):
        k0 = pl.multiple_of(kb * kb_sz, kb_sz)
        bias = sc_ref[prv, kb]
        for h in range(N_HEADS):
            hs = slice(h * HEAD_DIM, (h + 1) * HEAD_DIM)
            s_ref[h] = lax.dot_general(k_ref[pl.ds(k0, kb_sz), hs], q_ref[:, hs],
                                       (((1,), (1,)), ((), ())),
                                       preferred_element_type=F32) + bias

    def att_update(kb):
        for h in range(N_HEADS):
            s = s_ref[h]
            m_old = m_ref[h]
            m_new = jnp.maximum(m_old, jnp.max(s, axis=0, keepdims=True))
            p = jnp.exp2(s - m_new)
            p_ref[h] = p.astype(BF16)
            a_ref[h] = jnp.exp2(m_old - m_new)
            m_ref[h] = m_new
        ones = jnp.ones((2 * SUBLANES, kb_sz), BF16)
        for h in range(N_HEADS):
            hs = slice(h * HEAD_DIM, (h + 1) * HEAD_DIM)
            pv = jnp.dot(jnp.concatenate([vt_ref[kb, hs, :], ones], axis=0), p_ref[h],
                         preferred_element_type=F32)
            acc_ref[h] = a_ref[h] * acc_ref[h] + pv[:HEAD_DIM]
            l_ref[h] = a_ref[h] * l_ref[h] + pv[HEAD_DIM:HEAD_DIM + 1]

    m_ref[...] = jnp.full(m_ref.shape, NEG_BIG, F32)
    l_ref[...] = jnp.zeros(l_ref.shape, F32)
    acc_ref[...] = jnp.zeros(acc_ref.shape, F32)

    def both(kb, _):
        att_logits(kb)
        score_block(kb)
        att_update(kb)
        return 0

    def only_score(kb, _):
        score_block(kb)
        return 0

    def only_att(kb, _):
        att_logits(kb)
        att_update(kb)
        return 0

    @pl.when(has_score)
    def _():
        lax.fori_loop(0, n_kb_att, both, 0)
        lax.fori_loop(n_kb_att, n_kb, only_score, 0)

    @pl.when(jnp.logical_not(has_score))
    def _():
        lax.fori_loop(0, n_kb_att, only_att, 0)

    @pl.when(has_att)
    def _():
        ssq = jnp.zeros((1, tq), F32)
        for h in range(N_HEADS):
            yh = acc_ref[h] / l_ref[h]
            acc_ref[h] = yh
            ssq = ssq + jnp.sum(yh * yh, axis=0, keepdims=True)
        scale = lax.rsqrt(ssq / ATTN_DIM + RMS_EPS)
        for h in range(N_HEADS):
            hs = slice(h * HEAD_DIM, (h + 1) * HEAD_DIM)
            o_ref[:, hs] = ((acc_ref[h] * scale).T * gn_ref[:, hs]).astype(BF16)

    @pl.when(has_score)
    def _():
        _select_keys(sc_ref, cur, i, n_kb, admissible, key_row, topk)


def _select_keys(sc_ref, cur, i, n_kb, admissible, key_row, topk):
    tq, kb_sz = ATT_TQ, ATT_KB

    @pl.when(n_kb % 2 == 1)
    def _():
        sc_ref[cur, n_kb] = jnp.full((kb_sz, tq), -jnp.inf, F32)

    def count_if(pred):
        def one(kb, cnt):
            hit = jnp.where(pred(sc_ref[cur, kb], kb * kb_sz), 1.0, 0.0)
            return cnt + jnp.sum(hit.reshape(kb_sz // SUBLANES, SUBLANES, tq), axis=0)

        def pair(j, cnt):
            return one(2 * j + 1, one(2 * j, cnt))

        cnt = lax.fori_loop(0, (n_kb + 1) // 2, pair, jnp.zeros((SUBLANES, tq), F32))
        return jnp.sum(cnt, axis=0, keepdims=True)

    k_f = jnp.float32(topk)
    q_lane = lax.broadcasted_iota(jnp.int32, (1, tq), 1)
    n_adm = ((i * (tq // CHUNK) + q_lane // CHUNK + 1) * CHUNK).astype(F32)
    searched = n_adm > k_f
    c0 = count_if(lambda s, _: s >= 0.0)
    pos = c0 >= k_f
    thr0 = jnp.where(pos, jnp.int32(0), jnp.int32(-2 ** 31))
    cnt0 = jnp.where(pos, c0, (n_kb * kb_sz).astype(F32))

    def unresolved(cnt_thr):
        return jnp.max(jnp.where(searched, cnt_thr, k_f)) > k_f

    def bisect_cond(st):
        b, _, cnt_thr = st
        return jnp.logical_and(b < 31, unresolved(cnt_thr))

    def bisect_step(b, thr, cnt_thr):
        cand = thr + (jnp.int32(1) << jnp.maximum(30 - b, 0))
        cand_f = _key_to_f32(cand)
        c = count_if(lambda s, _: s >= cand_f)
        ok = jnp.logical_and(c >= k_f, b < 31)
        return jnp.where(ok, cand, thr), jnp.where(ok, c, cnt_thr)

    def bisect(st):
        b, thr, cnt_thr = st
        thr, cnt_thr = bisect_step(b, thr, cnt_thr)
        thr, cnt_thr = bisect_step(b + 1, thr, cnt_thr)
        return b + 2, thr, cnt_thr

    _, thr, cnt_thr = lax.while_loop(bisect_cond, bisect, (jnp.int32(0), thr0, cnt0))
    thr_f = jnp.where(searched, _key_to_f32(thr), -jnp.inf)

    def tie_cut(_):
        need = k_f - count_if(lambda s, _: s > thr_f)
        n_bits = (sc_ref.shape[1] * kb_sz - 1).bit_length()

        def body(b, m):
            step = jnp.int32(1) << (n_bits - 1 - b)
            top = m + step - 1
            c = count_if(lambda s, k0: jnp.logical_and(s == thr_f, key_row + k0 <= top))
            return jnp.where(c < need, m + step, m)

        return lax.fori_loop(0, n_bits, body, jnp.zeros((1, tq), jnp.int32))

    idx_cut = lax.cond(unresolved(cnt_thr), tie_cut,
                       lambda _: jnp.full((1, tq), 2 ** 30, jnp.int32), 0)

    def bias_block(kb, _):
        s = sc_ref[cur, kb]
        sel = jnp.logical_or(s > thr_f, jnp.logical_and(s == thr_f, key_row + kb * kb_sz <= idx_cut))
        sel = jnp.logical_and(sel, admissible(kb))
        sc_ref[cur, kb] = jnp.where(sel, 0.0, -jnp.inf)
        return 0

    lax.fori_loop(0, n_kb, bias_block, 0)


def attn_group(q, k, vt, qi, kk, wi, gn, nb, seq):
    tq = ATT_TQ
    tiles = seq // tq
    topk = min(TOPK_MAX, seq // 4)
    assert topk <= ATT_KB and ATT_TQ % ATT_KB == 0 and ATT_KB % CHUNK == 0 and (seq // ATT_KB) % 2 == 0
    att_row = lambda w: pl.BlockSpec((tq, w), lambda b, i: (b * tiles + jnp.maximum(i - 1, 0), 0))
    sc_row = lambda w: pl.BlockSpec((tq, w), lambda b, i: (b * tiles + jnp.minimum(i, tiles - 1), 0))
    whole = lambda w: pl.BlockSpec((seq, w), lambda b, i: (b, 0), pipeline_mode=pl.Buffered(1))
    vt_spec = pl.BlockSpec((None, seq // ATT_KB, ATTN_DIM, ATT_KB), lambda b, i: (b, 0, 0, 0),
                           pipeline_mode=pl.Buffered(1))
    return pl.pallas_call(
        functools.partial(_attn_kernel, topk=topk, n_tiles=tiles),
        grid=(nb, tiles + 1),
        in_specs=[att_row(ATTN_DIM), sc_row(IDX_HEADS * IDX_DIM), sc_row(LANES),
                  whole(ATTN_DIM), vt_spec, whole(LANES),
                  pl.BlockSpec((1, ATTN_DIM), lambda b, i: (0, 0))],
        out_specs=att_row(ATTN_DIM),
        out_shape=jax.ShapeDtypeStruct((nb * seq, ATTN_DIM), BF16),
        scratch_shapes=[pltpu.VMEM((2, seq // ATT_KB, ATT_KB, tq), F32),
                        pltpu.VMEM((IDX_HEADS, tq, LANES), F32),
                        pltpu.VMEM((N_HEADS, 1, tq), F32),
                        pltpu.VMEM((N_HEADS, 1, tq), F32),
                        pltpu.VMEM((N_HEADS, 1, tq), F32),
                        pltpu.VMEM((N_HEADS, HEAD_DIM, tq), F32),
                        pltpu.VMEM((N_HEADS, ATT_KB, tq), F32),
                        pltpu.VMEM((N_HEADS, ATT_KB, tq), BF16)],
        compiler_params=_cparams(("arbitrary", "arbitrary"), VMEM_LIMIT_ATTN),
        name="attn",
    )(q, qi, wi, k, vt, kk, gn.reshape(1, -1))


def _layer_norm(r, g, b):
    mu = jnp.mean(r, axis=-1, keepdims=True)
    d = r - mu
    var = jnp.mean(d * d, axis=-1, keepdims=True)
    return d * lax.rsqrt(var + LN_EPS) * g + b


def _out_proj_kernel(x_ref, ys_ref, yc_ref, ya_ref, w_ref, gate_ref, g_ref, b_ref, o_ref):
    half = x_ref.shape[0] // 2
    for r0 in (0, half):
        rows = pl.ds(r0, half)
        mix = jnp.dot(ys_ref[rows, :], w_ref[pl.ds(0, SSM_DIM), :], preferred_element_type=F32)
        mix += jnp.dot(yc_ref[rows, :], w_ref[pl.ds(SSM_DIM, CONV_DIM), :],
                       preferred_element_type=F32)
        mix += jnp.dot(ya_ref[rows, :], w_ref[pl.ds(SSM_DIM + CONV_DIM, ATTN_DIM), :],
                       preferred_element_type=F32)
        r = ALPHA * x_ref[rows, :] + gate_ref[0] * mix
        o_ref[rows, :] = _layer_norm(r, g_ref[...], b_ref[...])


def out_proj(x, ys, yc, ya, w_o, gate, ln_g, ln_b):
    nb, seq, _ = x.shape
    tm = OUT_TM
    tiles = seq // tm
    row = lambda w: pl.BlockSpec((tm, w), lambda i: (i, 0))
    xrow = pl.BlockSpec((None, tm, D_MODEL), lambda i: (i // tiles, i % tiles, 0))
    vec = pl.BlockSpec((1, D_MODEL), lambda i: (0, 0))
    return pl.pallas_call(
        _out_proj_kernel,
        grid=(nb * tiles,),
        in_specs=[xrow, row(SSM_DIM), row(CONV_DIM), row(ATTN_DIM),
                  _resident((D_MODEL, D_MODEL)),
                  pl.BlockSpec((1, 1, D_MODEL), lambda i: (i // tiles, 0, 0)), vec, vec],
        out_specs=xrow,
        out_shape=jax.ShapeDtypeStruct((nb, seq, D_MODEL), F32),
        compiler_params=_cparams(("arbitrary",)),
        name="out_proj",
    )(x, ys, yc, ya, w_o.astype(BF16), gate, ln_g.reshape(1, -1), ln_b.reshape(1, -1))


def _ffn_kernel(x_ref, sc_ref, sh_ref, w1_ref, w2_ref, gate_ref, g_ref, b_ref, o_ref,
                h_ref, acc_ref):
    j = pl.program_id(1)

    @pl.when(j == 0)
    def _():
        h_ref[...] = (x_ref[...] * (1.0 + sc_ref[0]) + sh_ref[0]).astype(BF16)
        acc_ref[...] = jnp.zeros_like(acc_ref)

    a = jnp.maximum(jnp.dot(h_ref[...], w1_ref[...], preferred_element_type=F32), 0.0)
    a = (a * a).astype(BF16)
    for c in range(D_MODEL // FFN_TN):
        cs = slice(c * FFN_TN, (c + 1) * FFN_TN)
        acc_ref[:, cs] += jnp.dot(a, w2_ref[:, cs], preferred_element_type=F32)

    @pl.when(j == pl.num_programs(1) - 1)
    def _():
        r = ALPHA * x_ref[...] + gate_ref[0] * acc_ref[...]
        o_ref[...] = _layer_norm(r, g_ref[...], b_ref[...])


def ffn(x, sc, sh, w1_bf16, w2_bf16, layer, gate, ln_g, ln_b):
    nb, seq, _ = x.shape
    tm, tf = FFN_TM, FFN_TF
    tiles = seq // tm
    row = pl.BlockSpec((None, tm, D_MODEL), lambda i, j: (i // tiles, i % tiles, 0))
    mod = pl.BlockSpec((1, 1, D_MODEL), lambda i, j: (i // tiles, 0, 0))
    vec = pl.BlockSpec((1, D_MODEL), lambda i, j: (0, 0))
    return pl.pallas_call(
        _ffn_kernel,
        grid=(nb * tiles, D_FF // tf),
        in_specs=[row, mod, mod,
                  pl.BlockSpec((None, D_MODEL, tf), lambda i, j: (layer, 0, j)),
                  pl.BlockSpec((None, tf, D_MODEL), lambda i, j: (layer, j, 0)),
                  mod, vec, vec],
        out_specs=row,
        out_shape=jax.ShapeDtypeStruct((nb, seq, D_MODEL), F32),
        scratch_shapes=[pltpu.VMEM((tm, D_MODEL), BF16), pltpu.VMEM((tm, D_MODEL), F32)],
        compiler_params=_cparams(("arbitrary", "arbitrary")),
        name="ffn",
    )(x, sc, sh, w1_bf16, w2_bf16, gate, ln_g.reshape(1, -1), ln_b.reshape(1, -1))


def kernel(x, c, w_ada, b_ada, w_in, lam_re, lam_im, log_dt, ssm_b_re, ssm_b_im, ssm_c_re, ssm_c_im,
           ssm_d, w_glu, b_glu, conv_w, gnorm_g, w_o, ln1_g, ln1_b, w_ff1, w_ff2, ln2_g, ln2_b):
    nb, seq, _ = x.shape
    tables = rope_tables(seq)
    mods = adaln(c, w_ada, b_ada)
    w_in_bf16 = w_in.astype(BF16)
    w1_bf16, w2_bf16 = w_ff1.astype(BF16), w_ff2.astype(BF16)
    for l in range(DEPTH):
        sh1, sc1, g1, sh2, sc2, g2 = (m[:, None, :] for m in jnp.split(mods[l], 6, axis=-1))
        gn = gnorm_g[l]
        u, yc, q, k, vt, qi, kk, wi = in_proj(x, sc1, sh1, w_in_bf16, l, tables, conv_w[l],
                                              gn[SSM_DIM:SSM_DIM + CONV_DIM])
        ys = s5_group(u.reshape(nb, seq, SSM_DIM), lam_re[l], lam_im[l], log_dt[l], ssm_b_re[l],
                      ssm_b_im[l], ssm_c_re[l], ssm_c_im[l], ssm_d[l], w_glu[l], b_glu[l],
                      gn[:SSM_DIM]).reshape(nb * seq, SSM_DIM)
        ya = attn_group(q, k, vt, qi, kk, wi, gn[SSM_DIM + CONV_DIM:], nb, seq)
        x = out_proj(x, ys, yc, ya, w_o[l], g1, ln1_g[l], ln1_b[l])
        x = ffn(x, sc2, sh2, w1_bf16, w2_bf16, l, g2, ln2_g[l], ln2_b[l])
    return x
```

```python
import functools
import math

import jax
import jax.numpy as jnp
from jax import lax
from jax.experimental import pallas as pl
from jax.experimental.pallas import tpu as pltpu

D_MODEL = 2048
DEPTH = 2
CHUNK = 64
SSM_DIM = 512
SSM_GROUP_CH = 16
SSM_GROUPS = 32
SSM_STATE = 64
CONV_DIM = 512
CONV_WIDTH = 3
ATTN_DIM = 1024
HEAD_DIM = 128
N_HEADS = 8
IDX_HEADS = 16
IDX_DIM = 64
TOPK_MAX = 256
D_FF = 4 * D_MODEL
ROPE_THETA = 10000.0
ALPHA = (2.0 * DEPTH) ** 0.25
LN_EPS = 1e-5
RMS_EPS = 1e-6

_in_edges = [0, SSM_DIM, SSM_DIM + 3 * CONV_DIM]
_in_edges += [_in_edges[-1] + ATTN_DIM, _in_edges[-1] + 2 * ATTN_DIM, _in_edges[-1] + 3 * ATTN_DIM]
_in_edges += [_in_edges[-1] + IDX_HEADS * IDX_DIM]
_in_edges += [_in_edges[-1] + IDX_DIM + IDX_HEADS]
IN_COLS = {name: (_in_edges[j], _in_edges[j + 1])
           for j, name in enumerate(("u", "conv", "q", "k", "v", "qi", "tail"))}

LANES = 128
SUBLANES = 8
VMEM_CAPACITY = 64 * 1024 * 1024
VMEM_LIMIT = 56 * 1024 * 1024
VMEM_LIMIT_ATTN = VMEM_CAPACITY - 4 * 1024 * 1024

ADA_TN = 1024
PROJ_TM = 256
S5_TC = 128
S5_HALF = SSM_DIM // 2
S5_NH = SSM_GROUPS // 2 * SSM_STATE
ATT_TQ = 512
ATT_KB = 256
OUT_TM = 512
FFN_TM = 512
FFN_TF = 1024
FFN_TN = 512

BF16 = jnp.bfloat16
F32 = jnp.float32
NEG_BIG = -1e30
Q_SCALE = HEAD_DIM ** -0.5 * math.log2(math.e)


def _cparams(sem, vmem_limit=VMEM_LIMIT):
    return pltpu.CompilerParams(dimension_semantics=sem, vmem_limit_bytes=vmem_limit)


def _resident(shape):
    nd = len(shape)
    return pl.BlockSpec(shape, lambda *_: (0,) * nd, pipeline_mode=pl.Buffered(1))


def _adaln_kernel(c_ref, w_ref, b_ref, o_ref):
    w = w_ref[0].astype(BF16)
    o_ref[0] = jnp.dot(c_ref[...], w, preferred_element_type=F32) + b_ref[0]


def adaln(c, w_ada, b_ada):
    nb = c.shape[0]
    rows = 16
    cp = jnp.zeros((rows, D_MODEL), BF16).at[:nb].set(c.astype(BF16))
    n_out = w_ada.shape[-1]
    out = pl.pallas_call(
        _adaln_kernel,
        grid=(DEPTH, n_out // ADA_TN),
        in_specs=[
            pl.BlockSpec((rows, D_MODEL), lambda l, j: (0, 0)),
            pl.BlockSpec((1, D_MODEL, ADA_TN), lambda l, j: (l, 0, j)),
            pl.BlockSpec((1, 1, ADA_TN), lambda l, j: (l, 0, j)),
        ],
        out_specs=pl.BlockSpec((1, rows, ADA_TN), lambda l, j: (l, 0, j)),
        out_shape=jax.ShapeDtypeStruct((DEPTH, rows, n_out), F32),
        compiler_params=_cparams(("arbitrary", "arbitrary")),
        name="adaln",
    )(cp, w_ada, b_ada.reshape(DEPTH, 1, n_out))
    return out[:, :nb]


def _rope_halves(x, cos, sin_signed, half):
    if 2 * half == LANES:
        swapped = pltpu.roll(x, half, axis=1)
    else:
        lane = lax.broadcasted_iota(jnp.int32, x.shape, 1)
        first = (lane % (2 * half)) < half
        swapped = jnp.where(first, pltpu.roll(x, LANES - half, axis=1), pltpu.roll(x, half, axis=1))
    return x * cos + swapped * sin_signed


def _short_conv(c, w_ref, gn_ref, o_ref, zp_ref, *, first):
    tm = c.shape[0]
    ch, gb, gc = c[:, :CONV_DIM], c[:, CONV_DIM:2 * CONV_DIM], c[:, 2 * CONV_DIM:]

    @pl.when(pl.program_id(0) == 0)
    def _():
        zp_ref[pl.ds(tm, SUBLANES), :] = jnp.zeros((SUBLANES, CONV_DIM), F32)

    zp_ref[pl.ds(0, SUBLANES), :] = jnp.where(first, 0.0, zp_ref[pl.ds(tm, SUBLANES), :])
    zp_ref[pl.ds(SUBLANES, tm), :] = gc * ch
    acc = zp_ref[pl.ds(SUBLANES, tm), :] * w_ref[2:3, :]
    acc += zp_ref[pl.ds(SUBLANES - 1, tm), :] * w_ref[1:2, :]
    acc += zp_ref[pl.ds(SUBLANES - 2, tm), :] * w_ref[0:1, :]
    y = gb * acc
    y = y * lax.rsqrt(jnp.mean(y * y, axis=-1, keepdims=True) + RMS_EPS) * gn_ref[...]
    o_ref[...] = y.astype(BF16)


def _in_proj_kernel(x_ref, sc_ref, sh_ref, w_ref, wv_ref, wt_ref,
                    cosa_ref, sina_ref, cosi_ref, sini_ref, cw_ref, cg_ref,
                    u_ref, yc_ref, q_ref, k_ref, vt_ref, qi_ref, kk_ref, wi_ref, zp_ref,
                    *, tiles_per_seq):
    def proj(name):
        lo, hi = IN_COLS[name]
        return jnp.dot(h, w_ref[:, lo:hi], preferred_element_type=F32)

    h = (x_ref[...] * (1.0 + sc_ref[0]) + sh_ref[0]).astype(BF16)
    u_ref[...] = proj("u")
    _short_conv(proj("conv"), cw_ref, cg_ref, yc_ref, zp_ref,
                first=pl.program_id(0) % tiles_per_seq == 0)
    vt_ref[...] = lax.dot_general(wv_ref[...], h, (((1,), (1,)), ((), ())),
                                  preferred_element_type=F32).astype(BF16)

    cosa, sina = cosa_ref[...], sina_ref[...]
    q = proj("q")
    k = proj("k")
    for hd in range(N_HEADS):
        sl = slice(hd * HEAD_DIM, (hd + 1) * HEAD_DIM)
        q_ref[:, sl] = (_rope_halves(q[:, sl], cosa, sina, HEAD_DIM // 2)
                        * Q_SCALE).astype(BF16)
        k_ref[:, sl] = _rope_halves(k[:, sl], cosa, sina, HEAD_DIM // 2).astype(BF16)

    cosi, sini = cosi_ref[...], sini_ref[...]
    qi = proj("qi")
    for g in range(IDX_HEADS * IDX_DIM // LANES):
        sl = slice(g * LANES, (g + 1) * LANES)
        qi_ref[:, sl] = _rope_halves(qi[:, sl], cosi, sini, IDX_DIM // 2).astype(BF16)

    tail = jnp.dot(h, wt_ref[...], preferred_element_type=F32)
    ki2 = jnp.where(lax.broadcasted_iota(jnp.int32, tail.shape, 1) < IDX_DIM,
                    tail, pltpu.roll(tail, IDX_DIM, axis=1))
    kk_ref[...] = _rope_halves(ki2, cosi, sini, IDX_DIM // 2).astype(BF16)
    wi_ref[...] = pltpu.roll(tail, LANES - IDX_DIM, axis=1) * (
        (IDX_DIM ** -0.5) * (IDX_HEADS ** -0.5))


def in_proj(x, sc, sh, w_in_bf16, layer, tables, conv_w, gn_conv):
    nb, seq, _ = x.shape
    n = nb * seq
    tm = PROJ_TM
    assert tm == ATT_KB
    tiles_per_seq = seq // tm
    lo, hi = IN_COLS["v"]
    wv = w_in_bf16[layer, :, lo:hi].T
    lo, hi = IN_COLS["tail"]
    wt = jnp.zeros((D_MODEL, LANES), BF16).at[:, :hi - lo].set(w_in_bf16[layer, :, lo:hi])
    main_cols = IN_COLS["tail"][0]
    cosa, sina, cosi, sini = tables

    row = lambda w: pl.BlockSpec((tm, w), lambda i: (i, 0))
    mod = pl.BlockSpec((1, 1, D_MODEL), lambda i: (i // tiles_per_seq, 0, 0))
    tab = pl.BlockSpec((tm, LANES), lambda i: (i % tiles_per_seq, 0))
    outs = pl.pallas_call(
        functools.partial(_in_proj_kernel, tiles_per_seq=tiles_per_seq),
        grid=(n // tm,),
        in_specs=[pl.BlockSpec((None, tm, D_MODEL), lambda i: (i // tiles_per_seq, i % tiles_per_seq, 0)),
                  mod, mod,
                  pl.BlockSpec((None, D_MODEL, main_cols), lambda i: (layer, 0, 0),
                               pipeline_mode=pl.Buffered(1)),
                  _resident(wv.shape), _resident(wt.shape),
                  tab, tab, tab, tab,
                  _resident((CONV_WIDTH, CONV_DIM)), _resident((1, CONV_DIM))],
        out_specs=[row(SSM_DIM), row(CONV_DIM), row(ATTN_DIM), row(ATTN_DIM),
                   pl.BlockSpec((None, None, ATTN_DIM, tm),
                                lambda i: (i // tiles_per_seq, i % tiles_per_seq, 0, 0)),
                   row(IDX_HEADS * IDX_DIM), row(LANES), row(LANES)],
        out_shape=[jax.ShapeDtypeStruct((n, SSM_DIM), F32),
                   jax.ShapeDtypeStruct((n, CONV_DIM), BF16),
                   jax.ShapeDtypeStruct((n, ATTN_DIM), BF16),
                   jax.ShapeDtypeStruct((n, ATTN_DIM), BF16),
                   jax.ShapeDtypeStruct((nb, tiles_per_seq, ATTN_DIM, tm), BF16),
                   jax.ShapeDtypeStruct((n, IDX_HEADS * IDX_DIM), BF16),
                   jax.ShapeDtypeStruct((n, LANES), BF16),
                   jax.ShapeDtypeStruct((n, LANES), F32)],
        scratch_shapes=[pltpu.VMEM((tm + SUBLANES, CONV_DIM), F32)],
        compiler_params=_cparams(("arbitrary",)),
        name="in_proj",
    )(x, sc, sh, w_in_bf16, wv, wt, cosa, sina, cosi, sini, conv_w, gn_conv.reshape(1, -1))
    return outs


def rope_tables(seq):
    def tab(dim):
        inv = 1.0 / (ROPE_THETA ** (jnp.arange(0, dim, 2, dtype=F32) / dim))
        ang = jnp.arange(seq, dtype=F32)[:, None] * inv[None, :]
        cos, sin = jnp.cos(ang), jnp.sin(ang)
        reps = LANES // dim
        return (jnp.tile(jnp.concatenate([cos, cos], axis=1), (1, reps)),
                jnp.tile(jnp.concatenate([-sin, sin], axis=1), (1, reps)))
    cosa, sina = tab(HEAD_DIM)
    cosi, sini = tab(IDX_DIM)
    return cosa, sina, cosi, sini


def _s5_kernel(u_ref, bb_ref, cc_ref, are_ref, aim_ref, d_ref, wg_ref, bg_ref, gn_ref,
               o_ref, st_ref, xs_ref, y_ref, lhs_ref, *, nb):
    rows_t = 2 * nb
    tc = xs_ref.shape[0] // rows_t
    n_slab = SSM_DIM // LANES

    def half_rows(b, s):
        return pl.ds(2 * b + s // (n_slab // 2), tc, stride=rows_t)

    @pl.when(pl.program_id(0) == 0)
    def _():
        st_ref[...] = jnp.zeros_like(st_ref)
        lhs_ref[...] = jnp.zeros_like(lhs_ref)

    for b in range(nb):
        for s in range(n_slab):
            lhs_ref[s, half_rows(b, s), :] = u_ref[b, :, s * LANES:(s + 1) * LANES]
    lhs = jnp.concatenate([lhs_ref[s] for s in range(n_slab)], axis=1).astype(BF16)
    xs_ref[...] = jnp.dot(lhs, bb_ref[...], preferred_element_type=F32)
    a_re, a_im = are_ref[...], aim_ref[...]

    def step(t, carry):
        xr, xi = carry
        r0 = pl.multiple_of(t * rows_t, rows_t)
        bur = xs_ref[pl.ds(r0, rows_t), :S5_NH]
        bui = xs_ref[pl.ds(r0, rows_t), S5_NH:]
        nr = a_re * xr - a_im * xi + bur
        ni = a_re * xi + a_im * xr + bui
        xs_ref[pl.ds(r0, rows_t), :S5_NH] = nr
        xs_ref[pl.ds(r0, rows_t), S5_NH:] = ni
        return nr, ni

    xr, xi = lax.fori_loop(0, tc, step, (st_ref[0], st_ref[1]), unroll=4)
    st_ref[0] = xr
    st_ref[1] = xi

    yy = jnp.dot(xs_ref[...].astype(BF16), cc_ref[...], preferred_element_type=F32)
    for s in range(n_slab):
        y_ref[s] = yy[:, s * LANES:(s + 1) * LANES]
    y = jnp.concatenate(
        [jnp.concatenate([y_ref[s, half_rows(b, s), :] for s in range(n_slab)], axis=1)
         for b in range(nb)], axis=0)
    y = y + d_ref[...] * u_ref[...].reshape(nb * tc, SSM_DIM)
    g = jax.nn.gelu(y)
    z = jnp.dot(g.astype(BF16), wg_ref[...], preferred_element_type=F32) + bg_ref[...]
    out = g * jax.nn.sigmoid(z)
    out = out * lax.rsqrt(jnp.mean(out * out, axis=-1, keepdims=True) + RMS_EPS) * gn_ref[...]
    o_ref[...] = out.reshape(nb, tc, SSM_DIM).astype(BF16)


def s5_group(u, lam_re, lam_im, log_dt, b_re, b_im, c_re, c_im, d_skip, w_glu, b_glu, gn):
    nb, seq, _ = u.shape
    assert 2 * nb == SUBLANES
    tc = S5_TC
    dt = jnp.exp(log_dt)[:, None]
    mag = jnp.exp(lam_re * dt)
    ang = lam_im * dt
    lb_re, lb_im = mag * jnp.cos(ang), mag * jnp.sin(ang)
    den = lam_re * lam_re + lam_im * lam_im
    n_re, n_im = lb_re - 1.0, lb_im
    f_re = (n_re * lam_re + n_im * lam_im) / den
    f_im = (n_im * lam_re - n_re * lam_im) / den
    bb_re = f_re[..., None] * b_re - f_im[..., None] * b_im
    bb_im = f_re[..., None] * b_im + f_im[..., None] * b_re
    gh = SSM_GROUPS // 2
    eye = jnp.eye(gh, dtype=F32)

    def in_mat(m):
        m = m.reshape(2, gh, SSM_STATE, SSM_GROUP_CH)
        bd = jnp.einsum('rgph,gk->rghkp', m, eye)
        return bd.reshape(SSM_DIM, S5_NH)

    def out_mat(m):
        m = m.reshape(2, gh, SSM_GROUP_CH, SSM_STATE)
        bd = jnp.einsum('rghp,gk->kprgh', m, eye)
        return bd.reshape(S5_NH, SSM_DIM)

    bb = jnp.concatenate([in_mat(bb_re), in_mat(bb_im)], axis=1).astype(BF16)
    cc = jnp.concatenate([out_mat(c_re), -out_mat(c_im)], axis=0).astype(BF16)

    def lane_vec(m):
        return jnp.tile(m.reshape(2, S5_NH), (nb, 1))

    a_re, a_im = lane_vec(lb_re), lane_vec(lb_im)


    rows = tc * 2 * nb
    out = pl.pallas_call(
        functools.partial(_s5_kernel, nb=nb),
        grid=(seq // tc,),
        in_specs=[
            pl.BlockSpec((nb, tc, SSM_DIM), lambda i: (0, i, 0)),
            _resident(bb.shape), _resident(cc.shape),
            _resident(a_re.shape), _resident(a_im.shape),
            _resident((1, SSM_DIM)), _resident((SSM_DIM, SSM_DIM)),
            _resident((1, SSM_DIM)), _resident((1, SSM_DIM)),
        ],
        out_specs=pl.BlockSpec((nb, tc, SSM_DIM), lambda i: (0, i, 0)),
        out_shape=jax.ShapeDtypeStruct((nb, seq, SSM_DIM), BF16),
        scratch_shapes=[pltpu.VMEM((2, 2 * nb, S5_NH), F32),
                        pltpu.VMEM((rows, 2 * S5_NH), F32),
                        pltpu.VMEM((SSM_DIM // LANES, rows, LANES), F32),
                        pltpu.VMEM((SSM_DIM // LANES, rows, LANES), F32)],
        compiler_params=_cparams(("arbitrary",)),
        name="s5",
    )(u, bb, cc, a_re, a_im, d_skip.reshape(1, -1), w_glu.astype(BF16),
      b_glu.reshape(1, -1), gn.reshape(1, -1))
    return out


def _key_to_f32(key):
    bits = key ^ ((key >> 31) & jnp.int32(0x7FFFFFFF))
    return pltpu.bitcast(bits, F32)


def _attn_kernel(q_ref, qi_ref, wi_ref, k_ref, vt_ref, kk_ref, gn_ref, o_ref,
                 sc_ref, thr_ref, cut_ref, wib_ref, m_ref, l_ref, a_ref, acc_ref, s_ref, p_ref,
                 *, topk, n_tiles):
    tq, kb_sz = ATT_TQ, ATT_KB
    per_tile = tq // kb_sz
    i = pl.program_id(1)
    has_score = i < n_tiles
    has_att = i > 0
    n_kb = (i + 1) * per_tile
    n_kb_att = i * per_tile
    cur = i % 2
    prv = 1 - cur

    for h in range(IDX_HEADS):
        wib_ref[h] = jnp.broadcast_to(wi_ref[:, h:h + 1], (tq, LANES))

    key_row = lax.broadcasted_iota(jnp.int32, (kb_sz, tq), 0)
    qry_lane = lax.broadcasted_iota(jnp.int32, (kb_sz, tq), 1)
    key_chunk = key_row // CHUNK
    qry_chunk = qry_lane // CHUNK + i * (tq // CHUNK)

    def admissible(kb):
        return key_chunk + kb * (kb_sz // CHUNK) <= qry_chunk

    def score_block(kb):
        k0 = pl.multiple_of(kb * kb_sz, kb_sz)
        kk = kk_ref[pl.ds(k0, kb_sz), :]
        lane = lax.broadcasted_iota(jnp.int32, kk.shape, 1)
        kk_half = [jnp.where((lane // IDX_DIM) == half, kk, jnp.zeros_like(kk)) for half in range(2)]
        acc = jnp.zeros((tq, kb_sz), F32)
        for h in list(range(0, IDX_HEADS, 2)) + list(range(1, IDX_HEADS, 2)):
            pair = qi_ref[:, (h // 2) * LANES:(h // 2 + 1) * LANES]
            logit = lax.dot_general(pair, kk_half[h % 2], (((1,), (1,)), ((), ())),
                                    preferred_element_type=F32)
            w = wib_ref[h]
            acc = acc + jnp.maximum(logit, 0.0) * jnp.concatenate([w] * (kb_sz // LANES), axis=1)
        sc_ref[cur, kb] = jnp.where(admissible(kb), acc.T, -jnp.inf)

    def att_logits(kb):
        k0 = pl.multiple_of(kb * kb_sz, kb_sz)
        sc = sc_ref[prv, kb]
        thr_f, idx_cut = thr_ref[prv], cut_ref[prv]
        sel = jnp.logical_or(sc > thr_f,
                             jnp.logical_and(sc == thr_f, key_row + kb * kb_sz <= idx_cut))
        sel = jnp.logical_and(sel, key_chunk + kb * (kb_sz // CHUNK) <= qry_chunk - tq // CHUNK)
        bias = jnp.where(sel, 0.0, -jnp.inf)
        for h in range(N_HEADS):
            hs = slice(h * HEAD_DIM, (h + 1) * HEAD_DIM)
            s_ref[h] = lax.dot_general(k_ref[pl.ds(k0, kb_sz), hs], q_ref[:, hs],
                                       (((1,), (1,)), ((), ())),
                                       preferred_element_type=F32) + bias

    def att_update(kb):
        for h in range(N_HEADS):
            s = s_ref[h]
            m_old = m_ref[h]
            m_new = jnp.maximum(m_old, jnp.max(s, axis=0, keepdims=True))
            p = jnp.exp2(s - m_new)
            p_ref[h] = p.astype(BF16)
            a_ref[h] = jnp.exp2(m_old - m_new)
            m_ref[h] = m_new
        ones = jnp.ones((2 * SUBLANES, kb_sz), BF16)
        for h in range(N_HEADS):
            hs = slice(h * HEAD_DIM, (h + 1) * HEAD_DIM)
            pv = jnp.dot(jnp.concatenate([vt_ref[kb, hs, :], ones], axis=0), p_ref[h],
                         preferred_element_type=F32)
            acc_ref[h] = a_ref[h] * acc_ref[h] + pv[:HEAD_DIM]
            l_ref[h] = a_ref[h] * l_ref[h] + pv[HEAD_DIM:HEAD_DIM + 1]

    m_ref[...] = jnp.full(m_ref.shape, NEG_BIG, F32)
    l_ref[...] = jnp.zeros(l_ref.shape, F32)
    acc_ref[...] = jnp.zeros(acc_ref.shape, F32)

    def both(kb, _):
        att_logits(kb)
        score_block(kb)
        att_update(kb)
        return 0

    def only_score(kb, _):
        score_block(kb)
        return 0

    def only_att(kb, _):
        att_logits(kb)
        att_update(kb)
        return 0

    @pl.when(has_score)
    def _():
        lax.fori_loop(0, n_kb_att, both, 0)
        lax.fori_loop(n_kb_att, n_kb, only_score, 0)

    @pl.when(jnp.logical_not(has_score))
    def _():
        lax.fori_loop(0, n_kb_att, only_att, 0)

    @pl.when(has_att)
    def _():
        ssq = jnp.zeros((1, tq), F32)
        for h in range(N_HEADS):
            yh = acc_ref[h] / l_ref[h]
            acc_ref[h] = yh
            ssq = ssq + jnp.sum(yh * yh, axis=0, keepdims=True)
        scale = lax.rsqrt(ssq / ATTN_DIM + RMS_EPS)
        for h in range(N_HEADS):
            hs = slice(h * HEAD_DIM, (h + 1) * HEAD_DIM)
            o_ref[:, hs] = ((acc_ref[h] * scale).T * gn_ref[:, hs]).astype(BF16)

    @pl.when(has_score)
    def _():
        _select_keys(sc_ref, thr_ref, cut_ref, cur, i, n_kb, key_row, topk)


def _select_keys(sc_ref, thr_ref, cut_ref, cur, i, n_kb, key_row, topk):
    tq, kb_sz = ATT_TQ, ATT_KB

    @pl.when(n_kb % 2 == 1)
    def _():
        sc_ref[cur, n_kb] = jnp.full((kb_sz, tq), -jnp.inf, F32)

    def count_if(pred):
        def one(kb, cnt):
            hit = jnp.where(pred(sc_ref[cur, kb], kb * kb_sz), 1.0, 0.0)
            return cnt + jnp.sum(hit.reshape(kb_sz // SUBLANES, SUBLANES, tq), axis=0)

        def pair(j, cnt):
            return one(2 * j + 1, one(2 * j, cnt))

        cnt = lax.fori_loop(0, (n_kb + 1) // 2, pair, jnp.zeros((SUBLANES, tq), F32))
        return jnp.sum(cnt, axis=0, keepdims=True)

    k_f = jnp.float32(topk)
    q_lane = lax.broadcasted_iota(jnp.int32, (1, tq), 1)
    n_adm = ((i * (tq // CHUNK) + q_lane // CHUNK + 1) * CHUNK).astype(F32)
    searched = n_adm > k_f
    c0 = count_if(lambda s, _: s >= 0.0)
    pos = c0 >= k_f
    thr0 = jnp.where(pos, jnp.int32(0), jnp.int32(-2 ** 31))
    cnt0 = jnp.where(pos, c0, (n_kb * kb_sz).astype(F32))

    def unresolved(cnt_thr):
        return jnp.max(jnp.where(searched, cnt_thr, k_f)) > k_f

    def bisect_cond(st):
        b, _, cnt_thr = st
        return jnp.logical_and(b < 31, unresolved(cnt_thr))

    def bisect_step(b, thr, cnt_thr):
        cand = thr + (jnp.int32(1) << jnp.maximum(30 - b, 0))
        cand_f = _key_to_f32(cand)
        c = count_if(lambda s, _: s >= cand_f)
        ok = jnp.logical_and(c >= k_f, b < 31)
        return jnp.where(ok, cand, thr), jnp.where(ok, c, cnt_thr)

    def bisect(st):
        b, thr, cnt_thr = st
        thr, cnt_thr = bisect_step(b, thr, cnt_thr)
        thr, cnt_thr = bisect_step(b + 1, thr, cnt_thr)
        return b + 2, thr, cnt_thr

    _, thr, cnt_thr = lax.while_loop(bisect_cond, bisect, (jnp.int32(0), thr0, cnt0))
    thr_f = jnp.where(searched, _key_to_f32(thr), -jnp.inf)

    def tie_cut(_):
        need = k_f - count_if(lambda s, _: s > thr_f)
        n_bits = (sc_ref.shape[1] * kb_sz - 1).bit_length()

        def body(b, m):
            step = jnp.int32(1) << (n_bits - 1 - b)
            top = m + step - 1
            c = count_if(lambda s, k0: jnp.logical_and(s == thr_f, key_row + k0 <= top))
            return jnp.where(c < need, m + step, m)

        return lax.fori_loop(0, n_bits, body, jnp.zeros((1, tq), jnp.int32))

    cut_ref[cur] = lax.cond(unresolved(cnt_thr), tie_cut,
                            lambda _: jnp.full((1, tq), 2 ** 30, jnp.int32), 0)
    thr_ref[cur] = thr_f


def attn_group(q, k, vt, qi, kk, wi, gn, nb, seq):
    tq = ATT_TQ
    tiles = seq // tq
    topk = min(TOPK_MAX, seq // 4)
    assert topk <= ATT_KB and ATT_TQ % ATT_KB == 0 and ATT_KB % CHUNK == 0 and (seq // ATT_KB) % 2 == 0
    att_row = lambda w: pl.BlockSpec((tq, w), lambda b, i: (b * tiles + jnp.maximum(i - 1, 0), 0))
    sc_row = lambda w: pl.BlockSpec((tq, w), lambda b, i: (b * tiles + jnp.minimum(i, tiles - 1), 0))
    whole = lambda w: pl.BlockSpec((seq, w), lambda b, i: (b, 0), pipeline_mode=pl.Buffered(1))
    vt_spec = pl.BlockSpec((None, seq // ATT_KB, ATTN_DIM, ATT_KB), lambda b, i: (b, 0, 0, 0),
                           pipeline_mode=pl.Buffered(1))
    return pl.pallas_call(
        functools.partial(_attn_kernel, topk=topk, n_tiles=tiles),
        grid=(nb, tiles + 1),
        in_specs=[att_row(ATTN_DIM), sc_row(IDX_HEADS * IDX_DIM), sc_row(LANES),
                  whole(ATTN_DIM), vt_spec, whole(LANES),
                  pl.BlockSpec((1, ATTN_DIM), lambda b, i: (0, 0))],
        out_specs=att_row(ATTN_DIM),
        out_shape=jax.ShapeDtypeStruct((nb * seq, ATTN_DIM), BF16),
        scratch_shapes=[pltpu.VMEM((2, seq // ATT_KB, ATT_KB, tq), F32),
                        pltpu.VMEM((2, 1, tq), F32),
                        pltpu.VMEM((2, 1, tq), jnp.int32),
                        pltpu.VMEM((IDX_HEADS, tq, LANES), F32),
                        pltpu.VMEM((N_HEADS, 1, tq), F32),
                        pltpu.VMEM((N_HEADS, 1, tq), F32),
                        pltpu.VMEM((N_HEADS, 1, tq), F32),
                        pltpu.VMEM((N_HEADS, HEAD_DIM, tq), F32),
                        pltpu.VMEM((N_HEADS, ATT_KB, tq), F32),
                        pltpu.VMEM((N_HEADS, ATT_KB, tq), BF16)],
        compiler_params=_cparams(("arbitrary", "arbitrary"), VMEM_LIMIT_ATTN),
        name="attn",
    )(q, qi, wi, k, vt, kk, gn.reshape(1, -1))


def _layer_norm(r, g, b):
    mu = jnp.mean(r, axis=-1, keepdims=True)
    d = r - mu
    var = jnp.mean(d * d, axis=-1, keepdims=True)
    return d * lax.rsqrt(var + LN_EPS) * g + b


def _out_proj_kernel(x_ref, ys_ref, yc_ref, ya_ref, w_ref, gate_ref, g_ref, b_ref, o_ref):
    half = x_ref.shape[0] // 2
    for r0 in (0, half):
        rows = pl.ds(r0, half)
        mix = jnp.dot(ys_ref[rows, :], w_ref[pl.ds(0, SSM_DIM), :], preferred_element_type=F32)
        mix += jnp.dot(yc_ref[rows, :], w_ref[pl.ds(SSM_DIM, CONV_DIM), :],
                       preferred_element_type=F32)
        mix += jnp.dot(ya_ref[rows, :], w_ref[pl.ds(SSM_DIM + CONV_DIM, ATTN_DIM), :],
                       preferred_element_type=F32)
        r = ALPHA * x_ref[rows, :] + gate_ref[0] * mix
        o_ref[rows, :] = _layer_norm(r, g_ref[...], b_ref[...])


def out_proj(x, ys, yc, ya, w_o, gate, ln_g, ln_b):
    nb, seq, _ = x.shape
    tm = OUT_TM
    tiles = seq // tm
    row = lambda w: pl.BlockSpec((tm, w), lambda i: (i, 0))
    xrow = pl.BlockSpec((None, tm, D_MODEL), lambda i: (i // tiles, i % tiles, 0))
    vec = pl.BlockSpec((1, D_MODEL), lambda i: (0, 0))
    return pl.pallas_call(
        _out_proj_kernel,
        grid=(nb * tiles,),
        in_specs=[xrow, row(SSM_DIM), row(CONV_DIM), row(ATTN_DIM),
                  _resident((D_MODEL, D_MODEL)),
                  pl.BlockSpec((1, 1, D_MODEL), lambda i: (i // tiles, 0, 0)), vec, vec],
        out_specs=xrow,
        out_shape=jax.ShapeDtypeStruct((nb, seq, D_MODEL), F32),
        compiler_params=_cparams(("arbitrary",)),
        name="out_proj",
    )(x, ys, yc, ya, w_o.astype(BF16), gate, ln_g.reshape(1, -1), ln_b.reshape(1, -1))


def _ffn_kernel(x_ref, sc_ref, sh_ref, w1_ref, w2_ref, gate_ref, g_ref, b_ref, o_ref,
                h_ref, acc_ref):
    j = pl.program_id(1)

    @pl.when(j == 0)
    def _():
        h_ref[...] = (x_ref[...] * (1.0 + sc_ref[0]) + sh_ref[0]).astype(BF16)
        acc_ref[...] = jnp.zeros_like(acc_ref)

    a = jnp.maximum(jnp.dot(h_ref[...], w1_ref[...], preferred_element_type=F32), 0.0)
    a = (a * a).astype(BF16)
    for c in range(D_MODEL // FFN_TN):
        cs = slice(c * FFN_TN, (c + 1) * FFN_TN)
        acc_ref[:, cs] += jnp.dot(a, w2_ref[:, cs], preferred_element_type=F32)

    @pl.when(j == pl.num_programs(1) - 1)
    def _():
        r = ALPHA * x_ref[...] + gate_ref[0] * acc_ref[...]
        o_ref[...] = _layer_norm(r, g_ref[...], b_ref[...])


def ffn(x, sc, sh, w1_bf16, w2_bf16, layer, gate, ln_g, ln_b):
    nb, seq, _ = x.shape
    tm, tf = FFN_TM, FFN_TF
    tiles = seq // tm
    row = pl.BlockSpec((None, tm, D_MODEL), lambda i, j: (i // tiles, i % tiles, 0))
    mod = pl.BlockSpec((1, 1, D_MODEL), lambda i, j: (i // tiles, 0, 0))
    vec = pl.BlockSpec((1, D_MODEL), lambda i, j: (0, 0))
    return pl.pallas_call(
        _ffn_kernel,
        grid=(nb * tiles, D_FF // tf),
        in_specs=[row, mod, mod,
                  pl.BlockSpec((None, D_MODEL, tf), lambda i, j: (layer, 0, j)),
                  pl.BlockSpec((None, tf, D_MODEL), lambda i, j: (layer, j, 0)),
                  mod, vec, vec],
        out_specs=row,
        out_shape=jax.ShapeDtypeStruct((nb, seq, D_MODEL), F32),
        scratch_shapes=[pltpu.VMEM((tm, D_MODEL), BF16), pltpu.VMEM((tm, D_MODEL), F32)],
        compiler_params=_cparams(("arbitrary", "arbitrary")),
        name="ffn",
    )(x, sc, sh, w1_bf16, w2_bf16, gate, ln_g.reshape(1, -1), ln_b.reshape(1, -1))


def kernel(x, c, w_ada, b_ada, w_in, lam_re, lam_im, log_dt, ssm_b_re, ssm_b_im, ssm_c_re, ssm_c_im,
           ssm_d, w_glu, b_glu, conv_w, gnorm_g, w_o, ln1_g, ln1_b, w_ff1, w_ff2, ln2_g, ln2_b):
    nb, seq, _ = x.shape
    tables = rope_tables(seq)
    mods = adaln(c, w_ada, b_ada)
    w_in_bf16 = w_in.astype(BF16)
    w1_bf16, w2_bf16 = w_ff1.astype(BF16), w_ff2.astype(BF16)
    for l in range(DEPTH):
        sh1, sc1, g1, sh2, sc2, g2 = (m[:, None, :] for m in jnp.split(mods[l], 6, axis=-1))
        gn = gnorm_g[l]
        u, yc, q, k, vt, qi, kk, wi = in_proj(x, sc1, sh1, w_in_bf16, l, tables, conv_w[l],
                                              gn[SSM_DIM:SSM_DIM + CONV_DIM])
        ys = s5_group(u.reshape(nb, seq, SSM_DIM), lam_re[l], lam_im[l], log_dt[l], ssm_b_re[l],
                      ssm_b_im[l], ssm_c_re[l], ssm_c_im[l], ssm_d[l], w_glu[l], b_glu[l],
                      gn[:SSM_DIM]).reshape(nb * seq, SSM_DIM)
        ya = attn_group(q, k, vt, qi, kk, wi, gn[SSM_DIM + CONV_DIM:], nb, seq)
        x = out_proj(x, ys, yc, ya, w_o[l], g1, ln1_g[l], ln1_b[l])
        x = ffn(x, sc2, sh2, w1_bf16, w2_bf16, l, g2, ln2_g[l], ln2_b[l])
    return x
```

```python
import functools
import math

import jax
import jax.numpy as jnp
from jax import lax
from jax.experimental import pallas as pl
from jax.experimental.pallas import tpu as pltpu

D_MODEL = 2048
DEPTH = 2
CHUNK = 64
SSM_DIM = 512
SSM_GROUP_CH = 16
SSM_GROUPS = 32
SSM_STATE = 64
CONV_DIM = 512
CONV_WIDTH = 3
ATTN_DIM = 1024
HEAD_DIM = 128
N_HEADS = 8
IDX_HEADS = 16
IDX_DIM = 64
TOPK_MAX = 256
D_FF = 4 * D_MODEL
ROPE_THETA = 10000.0
ALPHA = (2.0 * DEPTH) ** 0.25
LN_EPS = 1e-5
RMS_EPS = 1e-6

_in_edges = [0, SSM_DIM, SSM_DIM + 3 * CONV_DIM]
_in_edges += [_in_edges[-1] + ATTN_DIM, _in_edges[-1] + 2 * ATTN_DIM, _in_edges[-1] + 3 * ATTN_DIM]
_in_edges += [_in_edges[-1] + IDX_HEADS * IDX_DIM]
_in_edges += [_in_edges[-1] + IDX_DIM + IDX_HEADS]
IN_COLS = {name: (_in_edges[j], _in_edges[j + 1])
           for j, name in enumerate(("u", "conv", "q", "k", "v", "qi", "tail"))}

LANES = 128
SUBLANES = 8
VMEM_CAPACITY = 64 * 1024 * 1024
VMEM_LIMIT = 56 * 1024 * 1024
VMEM_LIMIT_ATTN = VMEM_CAPACITY - 4 * 1024 * 1024

ADA_TN = 1024
PROJ_TM = 256
S5_TC = 128
S5_HALF = SSM_DIM // 2
S5_NH = SSM_GROUPS // 2 * SSM_STATE
ATT_TQ = 512
ATT_KB = 256
OUT_TM = 512
FFN_TM = 512
FFN_TF = 1024
FFN_TN = 512
FFN_WBUF = 3

BF16 = jnp.bfloat16
F32 = jnp.float32
NEG_BIG = -1e30
Q_SCALE = HEAD_DIM ** -0.5 * math.log2(math.e)


def _cparams(sem, vmem_limit=VMEM_LIMIT):
    return pltpu.CompilerParams(dimension_semantics=sem, vmem_limit_bytes=vmem_limit)


def _resident(shape):
    nd = len(shape)
    return pl.BlockSpec(shape, lambda *_: (0,) * nd, pipeline_mode=pl.Buffered(1))


def _adaln_kernel(c_ref, w_ref, b_ref, o_ref):
    w = w_ref[0].astype(BF16)
    o_ref[0] = jnp.dot(c_ref[...], w, preferred_element_type=F32) + b_ref[0]


def adaln(c, w_ada, b_ada):
    nb = c.shape[0]
    rows = 16
    cp = jnp.zeros((rows, D_MODEL), BF16).at[:nb].set(c.astype(BF16))
    n_out = w_ada.shape[-1]
    out = pl.pallas_call(
        _adaln_kernel,
        grid=(DEPTH, n_out // ADA_TN),
        in_specs=[
            pl.BlockSpec((rows, D_MODEL), lambda l, j: (0, 0)),
            pl.BlockSpec((1, D_MODEL, ADA_TN), lambda l, j: (l, 0, j)),
            pl.BlockSpec((1, 1, ADA_TN), lambda l, j: (l, 0, j)),
        ],
        out_specs=pl.BlockSpec((1, rows, ADA_TN), lambda l, j: (l, 0, j)),
        out_shape=jax.ShapeDtypeStruct((DEPTH, rows, n_out), F32),
        compiler_params=_cparams(("arbitrary", "arbitrary")),
        name="adaln",
    )(cp, w_ada, b_ada.reshape(DEPTH, 1, n_out))
    return out[:, :nb]


def _rope_halves(x, cos, sin_signed, half):
    if 2 * half == LANES:
        swapped = pltpu.roll(x, half, axis=1)
    else:
        lane = lax.broadcasted_iota(jnp.int32, x.shape, 1)
        first = (lane % (2 * half)) < half
        swapped = jnp.where(first, pltpu.roll(x, LANES - half, axis=1), pltpu.roll(x, half, axis=1))
    return x * cos + swapped * sin_signed


def _short_conv(c, w_ref, gn_ref, o_ref, zp_ref, *, first):
    tm = c.shape[0]
    ch, gb, gc = c[:, :CONV_DIM], c[:, CONV_DIM:2 * CONV_DIM], c[:, 2 * CONV_DIM:]

    @pl.when(first)
    def _():
        zp_ref[pl.ds(0, SUBLANES), :] = jnp.zeros((SUBLANES, CONV_DIM), F32)

    @pl.when(jnp.logical_not(first))
    def _():
        zp_ref[pl.ds(0, SUBLANES), :] = zp_ref[pl.ds(tm, SUBLANES), :]

    zp_ref[pl.ds(SUBLANES, tm), :] = gc * ch
    acc = zp_ref[pl.ds(SUBLANES, tm), :] * w_ref[2:3, :]
    acc += zp_ref[pl.ds(SUBLANES - 1, tm), :] * w_ref[1:2, :]
    acc += zp_ref[pl.ds(SUBLANES - 2, tm), :] * w_ref[0:1, :]
    y = gb * acc
    y = y * lax.rsqrt(jnp.mean(y * y, axis=-1, keepdims=True) + RMS_EPS) * gn_ref[...]
    o_ref[...] = y.astype(BF16)


def _in_proj_kernel(x_ref, sc_ref, sh_ref, w_ref, wv_ref, wt_ref,
                    cosa_ref, sina_ref, cosi_ref, sini_ref, cw_ref, cg_ref,
                    u_ref, yc_ref, q_ref, k_ref, vt_ref, qi_ref, kk_ref, wi_ref, zp_ref,
                    *, tiles_per_seq):
    def proj(name):
        lo, hi = IN_COLS[name]
        return jnp.dot(h, w_ref[:, lo:hi], preferred_element_type=F32)

    h = (x_ref[...] * (1.0 + sc_ref[0]) + sh_ref[0]).astype(BF16)
    u_ref[...] = proj("u")
    _short_conv(proj("conv"), cw_ref, cg_ref, yc_ref, zp_ref,
                first=pl.program_id(0) % tiles_per_seq == 0)
    vt_ref[...] = lax.dot_general(wv_ref[...], h, (((1,), (1,)), ((), ())),
                                  preferred_element_type=F32).astype(BF16)

    cosa, sina = cosa_ref[...], sina_ref[...]
    q = proj("q")
    k = proj("k")
    for hd in range(N_HEADS):
        sl = slice(hd * HEAD_DIM, (hd + 1) * HEAD_DIM)
        q_ref[:, sl] = (_rope_halves(q[:, sl], cosa, sina, HEAD_DIM // 2)
                        * Q_SCALE).astype(BF16)
        k_ref[:, sl] = _rope_halves(k[:, sl], cosa, sina, HEAD_DIM // 2).astype(BF16)

    cosi, sini = cosi_ref[...], sini_ref[...]
    qi = proj("qi")
    for g in range(IDX_HEADS * IDX_DIM // LANES):
        sl = slice(g * LANES, (g + 1) * LANES)
        qi_ref[:, sl] = _rope_halves(qi[:, sl], cosi, sini, IDX_DIM // 2).astype(BF16)

    tail = jnp.dot(h, wt_ref[...], preferred_element_type=F32)
    ki2 = jnp.where(lax.broadcasted_iota(jnp.int32, tail.shape, 1) < IDX_DIM,
                    tail, pltpu.roll(tail, IDX_DIM, axis=1))
    kk_ref[...] = _rope_halves(ki2, cosi, sini, IDX_DIM // 2).astype(BF16)
    wi_ref[...] = pltpu.roll(tail, LANES - IDX_DIM, axis=1) * (
        (IDX_DIM ** -0.5) * (IDX_HEADS ** -0.5))


def in_proj(x, sc, sh, w_in_bf16, layer, tables, conv_w, gn_conv):
    nb, seq, _ = x.shape
    n = nb * seq
    tm = PROJ_TM
    assert tm == ATT_KB
    tiles_per_seq = seq // tm
    lo, hi = IN_COLS["v"]
    wv = w_in_bf16[layer, :, lo:hi].T
    lo, hi = IN_COLS["tail"]
    wt = jnp.zeros((D_MODEL, LANES), BF16).at[:, :hi - lo].set(w_in_bf16[layer, :, lo:hi])
    main_cols = IN_COLS["tail"][0]
    cosa, sina, cosi, sini = tables

    row = lambda w: pl.BlockSpec((tm, w), lambda i: (i, 0))
    mod = pl.BlockSpec((1, 1, D_MODEL), lambda i: (i // tiles_per_seq, 0, 0))
    tab = pl.BlockSpec((tm, LANES), lambda i: (i % tiles_per_seq, 0))
    outs = pl.pallas_call(
        functools.partial(_in_proj_kernel, tiles_per_seq=tiles_per_seq),
        grid=(n // tm,),
        in_specs=[pl.BlockSpec((None, tm, D_MODEL), lambda i: (i // tiles_per_seq, i % tiles_per_seq, 0)),
                  mod, mod,
                  pl.BlockSpec((None, D_MODEL, main_cols), lambda i: (layer, 0, 0),
                               pipeline_mode=pl.Buffered(1)),
                  _resident(wv.shape), _resident(wt.shape),
                  tab, tab, tab, tab,
                  _resident((CONV_WIDTH, CONV_DIM)), _resident((1, CONV_DIM))],
        out_specs=[row(SSM_DIM), row(CONV_DIM), row(ATTN_DIM), row(ATTN_DIM),
                   pl.BlockSpec((None, None, ATTN_DIM, tm),
                                lambda i: (i // tiles_per_seq, i % tiles_per_seq, 0, 0)),
                   row(IDX_HEADS * IDX_DIM), row(LANES), row(LANES)],
        out_shape=[jax.ShapeDtypeStruct((n, SSM_DIM), F32),
                   jax.ShapeDtypeStruct((n, CONV_DIM), BF16),
                   jax.ShapeDtypeStruct((n, ATTN_DIM), BF16),
                   jax.ShapeDtypeStruct((n, ATTN_DIM), BF16),
                   jax.ShapeDtypeStruct((nb, tiles_per_seq, ATTN_DIM, tm), BF16),
                   jax.ShapeDtypeStruct((n, IDX_HEADS * IDX_DIM), BF16),
                   jax.ShapeDtypeStruct((n, LANES), BF16),
                   jax.ShapeDtypeStruct((n, LANES), F32)],
        scratch_shapes=[pltpu.VMEM((tm + SUBLANES, CONV_DIM), F32)],
        compiler_params=_cparams(("arbitrary",)),
        name="in_proj",
    )(x, sc, sh, w_in_bf16, wv, wt, cosa, sina, cosi, sini, conv_w, gn_conv.reshape(1, -1))
    return outs


def rope_tables(seq):
    def tab(dim):
        inv = 1.0 / (ROPE_THETA ** (jnp.arange(0, dim, 2, dtype=F32) / dim))
        ang = jnp.arange(seq, dtype=F32)[:, None] * inv[None, :]
        cos, sin = jnp.cos(ang), jnp.sin(ang)
        reps = LANES // dim
        return (jnp.tile(jnp.concatenate([cos, cos], axis=1), (1, reps)),
                jnp.tile(jnp.concatenate([-sin, sin], axis=1), (1, reps)))
    cosa, sina = tab(HEAD_DIM)
    cosi, sini = tab(IDX_DIM)
    return cosa, sina, cosi, sini


def _s5_kernel(u_ref, bb_ref, cc_ref, are_ref, aim_ref, d_ref, wg_ref, bg_ref, gn_ref,
               o_ref, st_ref, xs_ref, y_ref, lhs_ref, *, nb):
    rows_t = 2 * nb
    tc = xs_ref.shape[0] // rows_t
    n_slab = SSM_DIM // LANES

    def half_rows(b, s):
        return pl.ds(2 * b + s // (n_slab // 2), tc, stride=rows_t)

    @pl.when(pl.program_id(0) == 0)
    def _():
        st_ref[...] = jnp.zeros_like(st_ref)
        lhs_ref[...] = jnp.zeros_like(lhs_ref)

    for b in range(nb):
        for s in range(n_slab):
            lhs_ref[s, half_rows(b, s), :] = u_ref[b, :, s * LANES:(s + 1) * LANES]
    lhs = jnp.concatenate([lhs_ref[s] for s in range(n_slab)], axis=1).astype(BF16)
    xs_ref[...] = jnp.dot(lhs, bb_ref[...], preferred_element_type=F32)
    a_re, a_im = are_ref[...], aim_ref[...]

    def step(t, carry):
        xr, xi = carry
        r0 = pl.multiple_of(t * rows_t, rows_t)
        bur = xs_ref[pl.ds(r0, rows_t), :S5_NH]
        bui = xs_ref[pl.ds(r0, rows_t), S5_NH:]
        nr = a_re * xr - a_im * xi + bur
        ni = a_re * xi + a_im * xr + bui
        xs_ref[pl.ds(r0, rows_t), :S5_NH] = nr
        xs_ref[pl.ds(r0, rows_t), S5_NH:] = ni
        return nr, ni

    xr, xi = lax.fori_loop(0, tc, step, (st_ref[0], st_ref[1]), unroll=4)
    st_ref[0] = xr
    st_ref[1] = xi

    yy = jnp.dot(xs_ref[...].astype(BF16), cc_ref[...], preferred_element_type=F32)
    for s in range(n_slab):
        y_ref[s] = yy[:, s * LANES:(s + 1) * LANES]
    y = jnp.concatenate(
        [jnp.concatenate([y_ref[s, half_rows(b, s), :] for s in range(n_slab)], axis=1)
         for b in range(nb)], axis=0)
    y = y + d_ref[...] * u_ref[...].reshape(nb * tc, SSM_DIM)
    g = jax.nn.gelu(y)
    z = jnp.dot(g.astype(BF16), wg_ref[...], preferred_element_type=F32) + bg_ref[...]
    out = g * jax.nn.sigmoid(z)
    out = out * lax.rsqrt(jnp.mean(out * out, axis=-1, keepdims=True) + RMS_EPS) * gn_ref[...]
    o_ref[...] = out.reshape(nb, tc, SSM_DIM).astype(BF16)


def s5_group(u, lam_re, lam_im, log_dt, b_re, b_im, c_re, c_im, d_skip, w_glu, b_glu, gn):
    nb, seq, _ = u.shape
    assert 2 * nb == SUBLANES
    tc = S5_TC
    dt = jnp.exp(log_dt)[:, None]
    mag = jnp.exp(lam_re * dt)
    ang = lam_im * dt
    lb_re, lb_im = mag * jnp.cos(ang), mag * jnp.sin(ang)
    den = lam_re * lam_re + lam_im * lam_im
    n_re, n_im = lb_re - 1.0, lb_im
    f_re = (n_re * lam_re + n_im * lam_im) / den
    f_im = (n_im * lam_re - n_re * lam_im) / den
    bb_re = f_re[..., None] * b_re - f_im[..., None] * b_im
    bb_im = f_re[..., None] * b_im + f_im[..., None] * b_re
    gh = SSM_GROUPS // 2
    eye = jnp.eye(gh, dtype=F32)

    def in_mat(m):
        m = m.reshape(2, gh, SSM_STATE, SSM_GROUP_CH)
        bd = jnp.einsum('rgph,gk->rghkp', m, eye)
        return bd.reshape(SSM_DIM, S5_NH)

    def out_mat(m):
        m = m.reshape(2, gh, SSM_GROUP_CH, SSM_STATE)
        bd = jnp.einsum('rghp,gk->kprgh', m, eye)
        return bd.reshape(S5_NH, SSM_DIM)

    bb = jnp.concatenate([in_mat(bb_re), in_mat(bb_im)], axis=1).astype(BF16)
    cc = jnp.concatenate([out_mat(c_re), -out_mat(c_im)], axis=0).astype(BF16)

    def lane_vec(m):
        return jnp.tile(m.reshape(2, S5_NH), (nb, 1))

    a_re, a_im = lane_vec(lb_re), lane_vec(lb_im)


    rows = tc * 2 * nb
    out = pl.pallas_call(
        functools.partial(_s5_kernel, nb=nb),
        grid=(seq // tc,),
        in_specs=[
            pl.BlockSpec((nb, tc, SSM_DIM), lambda i: (0, i, 0)),
            _resident(bb.shape), _resident(cc.shape),
            _resident(a_re.shape), _resident(a_im.shape),
            _resident((1, SSM_DIM)), _resident((SSM_DIM, SSM_DIM)),
            _resident((1, SSM_DIM)), _resident((1, SSM_DIM)),
        ],
        out_specs=pl.BlockSpec((nb, tc, SSM_DIM), lambda i: (0, i, 0)),
        out_shape=jax.ShapeDtypeStruct((nb, seq, SSM_DIM), BF16),
        scratch_shapes=[pltpu.VMEM((2, 2 * nb, S5_NH), F32),
                        pltpu.VMEM((rows, 2 * S5_NH), F32),
                        pltpu.VMEM((SSM_DIM // LANES, rows, LANES), F32),
                        pltpu.VMEM((SSM_DIM // LANES, rows, LANES), F32)],
        compiler_params=_cparams(("arbitrary",)),
        name="s5",
    )(u, bb, cc, a_re, a_im, d_skip.reshape(1, -1), w_glu.astype(BF16),
      b_glu.reshape(1, -1), gn.reshape(1, -1))
    return out


def _key_to_f32(key):
    bits = key ^ ((key >> 31) & jnp.int32(0x7FFFFFFF))
    return pltpu.bitcast(bits, F32)


def _attn_kernel(q_ref, qi_ref, wi_ref, k_ref, vt_ref, kk_ref, gn_ref, o_ref,
                 sc_ref, wib_ref, m_ref, l_ref, a_ref, acc_ref, s_ref, p_ref, *, topk, n_tiles):
    tq, kb_sz = ATT_TQ, ATT_KB
    per_tile = tq // kb_sz
    i = pl.program_id(1)
    has_score = i < n_tiles
    has_att = i > 0
    n_kb = (i + 1) * per_tile
    n_kb_att = i * per_tile
    cur = i % 2
    prv = 1 - cur

    for h in range(IDX_HEADS):
        wib_ref[h] = jnp.broadcast_to(wi_ref[:, h:h + 1], (tq, LANES))

    key_row = lax.broadcasted_iota(jnp.int32, (kb_sz, tq), 0)
    qry_lane = lax.broadcasted_iota(jnp.int32, (kb_sz, tq), 1)
    key_chunk = key_row // CHUNK
    qry_chunk = qry_lane // CHUNK + i * (tq // CHUNK)

    def admissible(kb):
        return key_chunk + kb * (kb_sz // CHUNK) <= qry_chunk

    def score_block(kb):
        k0 = pl.multiple_of(kb * kb_sz, kb_sz)
        kk = kk_ref[pl.ds(k0, kb_sz), :]
        lane = lax.broadcasted_iota(jnp.int32, kk.shape, 1)
        kk_half = [jnp.where((lane // IDX_DIM) == half, kk, jnp.zeros_like(kk)) for half in range(2)]
        acc = jnp.zeros((tq, kb_sz), F32)
        for h in list(range(0, IDX_HEADS, 2)) + list(range(1, IDX_HEADS, 2)):
            pair = qi_ref[:, (h // 2) * LANES:(h // 2 + 1) * LANES]
            logit = lax.dot_general(pair, kk_half[h % 2], (((1,), (1,)), ((), ())),
                                    preferred_element_type=F32)
            w = wib_ref[h]
            acc = acc + jnp.maximum(logit, 0.0) * jnp.concatenate([w] * (kb_sz // LANES), axis=1)
        sc_ref[cur, kb] = jnp.where(admissible(kb), acc.T, -jnp.inf)

    def att_logits(kb):
        k0 = pl.multiple_of(kb * kb_sz, kb_sz)
        bias = sc_ref[prv, kb]
        for h in range(N_HEADS):
            hs = slice(h * HEAD_DIM, (h + 1) * HEAD_DIM)
            s_ref[h] = lax.dot_general(k_ref[pl.ds(k0, kb_sz), hs], q_ref[:, hs],
                                       (((1,), (1,)), ((), ())),
                                       preferred_element_type=F32) + bias

    def att_update(kb):
        for h in range(N_HEADS):
            s = s_ref[h]
            m_old = m_ref[h]
            m_new = jnp.maximum(m_old, jnp.max(s, axis=0, keepdims=True))
            p = jnp.exp2(s - m_new)
            p_ref[h] = p.astype(BF16)
            a_ref[h] = jnp.exp2(m_old - m_new)
            m_ref[h] = m_new
        ones = jnp.ones((2 * SUBLANES, kb_sz), BF16)
        for h in range(N_HEADS):
            hs = slice(h * HEAD_DIM, (h + 1) * HEAD_DIM)
            pv = jnp.dot(jnp.concatenate([vt_ref[kb, hs, :], ones], axis=0), p_ref[h],
                         preferred_element_type=F32)
            acc_ref[h] = a_ref[h] * acc_ref[h] + pv[:HEAD_DIM]
            l_ref[h] = a_ref[h] * l_ref[h] + pv[HEAD_DIM:HEAD_DIM + 1]

    m_ref[...] = jnp.full(m_ref.shape, NEG_BIG, F32)
    l_ref[...] = jnp.zeros(l_ref.shape, F32)
    acc_ref[...] = jnp.zeros(acc_ref.shape, F32)

    def both(kb, _):
        att_logits(kb)
        score_block(kb)
        att_update(kb)
        return 0

    def only_score(kb, _):
        score_block(kb)
        return 0

    def only_att(kb, _):
        att_logits(kb)
        att_update(kb)
        return 0

    @pl.when(has_score)
    def _():
        lax.fori_loop(0, n_kb_att, both, 0)
        lax.fori_loop(n_kb_att, n_kb, only_score, 0)

    @pl.when(jnp.logical_not(has_score))
    def _():
        lax.fori_loop(0, n_kb_att, only_att, 0)

    @pl.when(has_att)
    def _():
        ssq = jnp.zeros((1, tq), F32)
        for h in range(N_HEADS):
            yh = acc_ref[h] / l_ref[h]
            acc_ref[h] = yh
            ssq = ssq + jnp.sum(yh * yh, axis=0, keepdims=True)
        scale = lax.rsqrt(ssq / ATTN_DIM + RMS_EPS)
        for h in range(N_HEADS):
            hs = slice(h * HEAD_DIM, (h + 1) * HEAD_DIM)
            o_ref[:, hs] = ((acc_ref[h] * scale).T * gn_ref[:, hs]).astype(BF16)

    @pl.when(has_score)
    def _():
        _select_keys(sc_ref, cur, i, n_kb, admissible, key_row, topk)


def _select_keys(sc_ref, cur, i, n_kb, admissible, key_row, topk):
    tq, kb_sz = ATT_TQ, ATT_KB

    @pl.when(n_kb % 2 == 1)
    def _():
        sc_ref[cur, n_kb] = jnp.full((kb_sz, tq), -jnp.inf, F32)

    def count_if(pred):
        def one(kb, cnt):
            hit = jnp.where(pred(sc_ref[cur, kb], kb * kb_sz), 1.0, 0.0)
            return cnt + jnp.sum(hit.reshape(kb_sz // SUBLANES, SUBLANES, tq), axis=0)

        def pair(j, cnt):
            return one(2 * j + 1, one(2 * j, cnt))

        cnt = lax.fori_loop(0, (n_kb + 1) // 2, pair, jnp.zeros((SUBLANES, tq), F32))
        return jnp.sum(cnt, axis=0, keepdims=True)

    k_f = jnp.float32(topk)
    q_lane = lax.broadcasted_iota(jnp.int32, (1, tq), 1)
    n_adm = ((i * (tq // CHUNK) + q_lane // CHUNK + 1) * CHUNK).astype(F32)
    searched = n_adm > k_f
    c0 = count_if(lambda s, _: s >= 0.0)
    pos = c0 >= k_f
    thr0 = jnp.where(pos, jnp.int32(0), jnp.int32(-2 ** 31))
    cnt0 = jnp.where(pos, c0, (n_kb * kb_sz).astype(F32))

    def unresolved(cnt_thr):
        return jnp.max(jnp.where(searched, cnt_thr, k_f)) > k_f

    def bisect_cond(st):
        b, _, cnt_thr = st
        return jnp.logical_and(b < 31, unresolved(cnt_thr))

    def bisect_step(b, thr, cnt_thr):
        cand = thr + (jnp.int32(1) << jnp.maximum(30 - b, 0))
        cand_f = _key_to_f32(cand)
        c = count_if(lambda s, _: s >= cand_f)
        ok = jnp.logical_and(c >= k_f, b < 31)
        return jnp.where(ok, cand, thr), jnp.where(ok, c, cnt_thr)

    def bisect(st):
        b, thr, cnt_thr = st
        thr, cnt_thr = bisect_step(b, thr, cnt_thr)
        thr, cnt_thr = bisect_step(b + 1, thr, cnt_thr)
        return b + 2, thr, cnt_thr

    _, thr, cnt_thr = lax.while_loop(bisect_cond, bisect, (jnp.int32(0), thr0, cnt0))
    thr_f = jnp.where(searched, _key_to_f32(thr), -jnp.inf)

    def tie_cut(_):
        need = k_f - count_if(lambda s, _: s > thr_f)
        n_bits = (sc_ref.shape[1] * kb_sz - 1).bit_length()

        def body(b, m):
            step = jnp.int32(1) << (n_bits - 1 - b)
            top = m + step - 1
            c = count_if(lambda s, k0: jnp.logical_and(s == thr_f, key_row + k0 <= top))
            return jnp.where(c < need, m + step, m)

        return lax.fori_loop(0, n_bits, body, jnp.zeros((1, tq), jnp.int32))

    idx_cut = lax.cond(unresolved(cnt_thr), tie_cut,
                       lambda _: jnp.full((1, tq), 2 ** 30, jnp.int32), 0)

    def bias_block(kb, _):
        s = sc_ref[cur, kb]
        sel = jnp.logical_or(s > thr_f, jnp.logical_and(s == thr_f, key_row + kb * kb_sz <= idx_cut))
        sel = jnp.logical_and(sel, admissible(kb))
        sc_ref[cur, kb] = jnp.where(sel, 0.0, -jnp.inf)
        return 0

    lax.fori_loop(0, n_kb, bias_block, 0)


def attn_group(q, k, vt, qi, kk, wi, gn, nb, seq):
    tq = ATT_TQ
    tiles = seq // tq
    topk = min(TOPK_MAX, seq // 4)
    assert topk <= ATT_KB and ATT_TQ % ATT_KB == 0 and ATT_KB % CHUNK == 0 and (seq // ATT_KB) % 2 == 0
    att_row = lambda w: pl.BlockSpec((tq, w), lambda b, i: (b * tiles + jnp.maximum(i - 1, 0), 0))
    sc_row = lambda w: pl.BlockSpec((tq, w), lambda b, i: (b * tiles + jnp.minimum(i, tiles - 1), 0))
    whole = lambda w: pl.BlockSpec((seq, w), lambda b, i: (b, 0), pipeline_mode=pl.Buffered(1))
    vt_spec = pl.BlockSpec((None, seq // ATT_KB, ATTN_DIM, ATT_KB), lambda b, i: (b, 0, 0, 0),
                           pipeline_mode=pl.Buffered(1))
    return pl.pallas_call(
        functools.partial(_attn_kernel, topk=topk, n_tiles=tiles),
        grid=(nb, tiles + 1),
        in_specs=[att_row(ATTN_DIM), sc_row(IDX_HEADS * IDX_DIM), sc_row(LANES),
                  whole(ATTN_DIM), vt_spec, whole(LANES),
                  pl.BlockSpec((1, ATTN_DIM), lambda b, i: (0, 0))],
        out_specs=att_row(ATTN_DIM),
        out_shape=jax.ShapeDtypeStruct((nb * seq, ATTN_DIM), BF16),
        scratch_shapes=[pltpu.VMEM((2, seq // ATT_KB, ATT_KB, tq), F32),
                        pltpu.VMEM((IDX_HEADS, tq, LANES), F32),
                        pltpu.VMEM((N_HEADS, 1, tq), F32),
                        pltpu.VMEM((N_HEADS, 1, tq), F32),
                        pltpu.VMEM((N_HEADS, 1, tq), F32),
                        pltpu.VMEM((N_HEADS, HEAD_DIM, tq), F32),
                        pltpu.VMEM((N_HEADS, ATT_KB, tq), F32),
                        pltpu.VMEM((N_HEADS, ATT_KB, tq), BF16)],
        compiler_params=_cparams(("arbitrary", "arbitrary"), VMEM_LIMIT_ATTN),
        name="attn",
    )(q, qi, wi, k, vt, kk, gn.reshape(1, -1))


def _layer_norm(r, g, b):
    mu = jnp.mean(r, axis=-1, keepdims=True)
    d = r - mu
    var = jnp.mean(d * d, axis=-1, keepdims=True)
    return d * lax.rsqrt(var + LN_EPS) * g + b


def _out_proj_kernel(x_ref, ys_ref, yc_ref, ya_ref, w_ref, gate_ref, g_ref, b_ref, o_ref):
    half = x_ref.shape[0] // 2
    for r0 in (0, half):
        rows = pl.ds(r0, half)
        mix = jnp.dot(ys_ref[rows, :], w_ref[pl.ds(0, SSM_DIM), :], preferred_element_type=F32)
        mix += jnp.dot(yc_ref[rows, :], w_ref[pl.ds(SSM_DIM, CONV_DIM), :],
                       preferred_element_type=F32)
        mix += jnp.dot(ya_ref[rows, :], w_ref[pl.ds(SSM_DIM + CONV_DIM, ATTN_DIM), :],
                       preferred_element_type=F32)
        r = ALPHA * x_ref[rows, :] + gate_ref[0] * mix
        o_ref[rows, :] = _layer_norm(r, g_ref[...], b_ref[...])


def out_proj(x, ys, yc, ya, w_o, gate, ln_g, ln_b):
    nb, seq, _ = x.shape
    tm = OUT_TM
    tiles = seq // tm
    row = lambda w: pl.BlockSpec((tm, w), lambda i: (i, 0))
    xrow = pl.BlockSpec((None, tm, D_MODEL), lambda i: (i // tiles, i % tiles, 0))
    vec = pl.BlockSpec((1, D_MODEL), lambda i: (0, 0))
    return pl.pallas_call(
        _out_proj_kernel,
        grid=(nb * tiles,),
        in_specs=[xrow, row(SSM_DIM), row(CONV_DIM), row(ATTN_DIM),
                  _resident((D_MODEL, D_MODEL)),
                  pl.BlockSpec((1, 1, D_MODEL), lambda i: (i // tiles, 0, 0)), vec, vec],
        out_specs=xrow,
        out_shape=jax.ShapeDtypeStruct((nb, seq, D_MODEL), F32),
        compiler_params=_cparams(("arbitrary",)),
        name="out_proj",
    )(x, ys, yc, ya, w_o.astype(BF16), gate, ln_g.reshape(1, -1), ln_b.reshape(1, -1))


def _ffn_kernel(x_ref, sc_ref, sh_ref, w1_hbm, w2_hbm, gate_ref, g_ref, b_ref, o_ref,
                h_ref, acc_ref, w1_buf, w2_buf, sem, *, layer):
    j = pl.program_id(1)
    n_j = pl.num_programs(1)
    total = pl.num_programs(0) * n_j
    s = pl.program_id(0) * n_j + j

    def tile_copies(step):
        jj, slot = step % n_j, step % FFN_WBUF
        return (pltpu.make_async_copy(w1_hbm.at[layer, jj], w1_buf.at[slot], sem.at[0, slot]),
                pltpu.make_async_copy(w2_hbm.at[layer, jj], w2_buf.at[slot], sem.at[1, slot]))

    @pl.when(s == 0)
    def _():
        for step in range(FFN_WBUF - 1):
            for cp in tile_copies(step):
                cp.start()

    @pl.when(s + FFN_WBUF - 1 < total)
    def _():
        for cp in tile_copies(s + FFN_WBUF - 1):
            cp.start()

    @pl.when(j == 0)
    def _():
        h_ref[...] = (x_ref[...] * (1.0 + sc_ref[0]) + sh_ref[0]).astype(BF16)
        acc_ref[...] = jnp.zeros_like(acc_ref)

    for cp in tile_copies(s):
        cp.wait()
    slot = s % FFN_WBUF
    a = jnp.maximum(jnp.dot(h_ref[...], w1_buf[slot], preferred_element_type=F32), 0.0)
    a = (a * a).astype(BF16)
    for c in range(D_MODEL // FFN_TN):
        cs = slice(c * FFN_TN, (c + 1) * FFN_TN)
        acc_ref[:, cs] += jnp.dot(a, w2_buf[slot, :, cs], preferred_element_type=F32)

    @pl.when(j == pl.num_programs(1) - 1)
    def _():
        r = ALPHA * x_ref[...] + gate_ref[0] * acc_ref[...]
        o_ref[...] = _layer_norm(r, g_ref[...], b_ref[...])


def ffn(x, sc, sh, w1_bf16, w2_bf16, layer, gate, ln_g, ln_b):
    nb, seq, _ = x.shape
    tm, tf = FFN_TM, FFN_TF
    tiles = seq // tm
    assert nb * tiles * (D_FF // tf) >= FFN_WBUF
    row = pl.BlockSpec((None, tm, D_MODEL), lambda i, j: (i // tiles, i % tiles, 0))
    mod = pl.BlockSpec((1, 1, D_MODEL), lambda i, j: (i // tiles, 0, 0))
    vec = pl.BlockSpec((1, D_MODEL), lambda i, j: (0, 0))
    return pl.pallas_call(
        functools.partial(_ffn_kernel, layer=layer),
        grid=(nb * tiles, D_FF // tf),
        in_specs=[row, mod, mod,
                  pl.BlockSpec(memory_space=pl.ANY), pl.BlockSpec(memory_space=pl.ANY),
                  mod, vec, vec],
        out_specs=row,
        out_shape=jax.ShapeDtypeStruct((nb, seq, D_MODEL), F32),
        scratch_shapes=[pltpu.VMEM((tm, D_MODEL), BF16), pltpu.VMEM((tm, D_MODEL), F32),
                        pltpu.VMEM((FFN_WBUF, D_MODEL, tf), BF16),
                        pltpu.VMEM((FFN_WBUF, tf, D_MODEL), BF16),
                        pltpu.SemaphoreType.DMA((2, FFN_WBUF))],
        compiler_params=_cparams(("arbitrary", "arbitrary")),
        name="ffn",
    )(x, sc, sh, w1_bf16, w2_bf16, gate, ln_g.reshape(1, -1), ln_b.reshape(1, -1))


def kernel(x, c, w_ada, b_ada, w_in, lam_re, lam_im, log_dt, ssm_b_re, ssm_b_im, ssm_c_re, ssm_c_im,
           ssm_d, w_glu, b_glu, conv_w, gnorm_g, w_o, ln1_g, ln1_b, w_ff1, w_ff2, ln2_g, ln2_b):
    nb, seq, _ = x.shape
    tables = rope_tables(seq)
    mods = adaln(c, w_ada, b_ada)
    w_in_bf16 = w_in.astype(BF16)
    n_ft = D_FF // FFN_TF
    w1_bf16 = jnp.swapaxes(w_ff1.astype(BF16).reshape(DEPTH, D_MODEL, n_ft, FFN_TF), 1, 2)
    w2_bf16 = w_ff2.astype(BF16).reshape(DEPTH, n_ft, FFN_TF, D_MODEL)
    for l in range(DEPTH):
        sh1, sc1, g1, sh2, sc2, g2 = (m[:, None, :] for m in jnp.split(mods[l], 6, axis=-1))
        gn = gnorm_g[l]
        u, yc, q, k, vt, qi, kk, wi = in_proj(x, sc1, sh1, w_in_bf16, l, tables, conv_w[l],
                                              gn[SSM_DIM:SSM_DIM + CONV_DIM])
        ys = s5_group(u.reshape(nb, seq, SSM_DIM), lam_re[l], lam_im[l], log_dt[l], ssm_b_re[l],
                      ssm_b_im[l], ssm_c_re[l], ssm_c_im[l], ssm_d[l], w_glu[l], b_glu[l],
                      gn[:SSM_DIM]).reshape(nb * seq, SSM_DIM)
        ya = attn_group(q, k, vt, qi, kk, wi, gn[SSM_DIM + CONV_DIM:], nb, seq)
        x = out_proj(x, ys, yc, ya, w_o[l], g1, ln1_g[l], ln1_b[l])
        x = ffn(x, sc2, sh2, w1_bf16, w2_bf16, l, g2, ln2_g[l], ln2_b[l])
    return x
```
